```python
import jax
import jax.numpy as jnp
from jax import lax
import numpy as np

D_MODEL = 1024
BATCH = 32
SEQ = 256
DEPTH = 2
DEC_BATCH = 2
DEC_SEQ = 4096
PAST_LEN = 256

GRID_W = 64
D_MIX = 1024
HEAD_DIM = 64
N_Q_HEADS = 4
N_KV_HEADS = 2
Q_PER_KV = N_Q_HEADS // N_KV_HEADS
ATTN_W = N_Q_HEADS * HEAD_DIM
KV_W = N_KV_HEADS * HEAD_DIM
Q_BLOCK = 128
ATTN_SCALE = HEAD_DIM ** -0.5
ROPE_THETA = 10000.0
CONV_W = 256
CONV_K = 31
SGU_W = 256
SGU_HEADS = 4
SGU_HEAD_W = SGU_W // SGU_HEADS
CHUNK = 128
FNET_W = 256
FNET_GROUPS = 4
FNET_GROUP_W = FNET_W // FNET_GROUPS
D_FF = 2816
FFN_CONV_K = 3
EPS = 1e-6
SPLIT_Q = ATTN_W
SPLIT_K = SPLIT_Q + KV_W
SPLIT_V = SPLIT_K + KV_W
SPLIT_CONV = SPLIT_V + 2 * CONV_W
SPLIT_SGU = SPLIT_CONV + 2 * SGU_W
IN_COLS = SPLIT_SGU + FNET_W
SPLITS = (SPLIT_Q, SPLIT_K, SPLIT_V, SPLIT_CONV, SPLIT_SGU)

kernel_name = 'hybrid_diffusion_prefix_step'


def rms_norm(x, g):
    xf = x.astype(jnp.float32)
    y = xf * lax.rsqrt(jnp.mean(xf * xf, axis=-1, keepdims=True) + EPS)
    return (y * g.astype(jnp.float32)).astype(x.dtype)


def layer_norm(x, g, b):
    xf = x.astype(jnp.float32)
    mu = jnp.mean(xf, axis=-1, keepdims=True)
    var = jnp.mean(jnp.square(xf - mu), axis=-1, keepdims=True)
    y = (xf - mu) * lax.rsqrt(var + EPS)
    return (y * g.astype(jnp.float32) + b.astype(jnp.float32)).astype(x.dtype)


def dwconv(x, w, b):
    y = lax.conv_general_dilated(
        x, w[:, None, :].astype(x.dtype), window_strides=(1,), padding='SAME',
        dimension_numbers=('NWC', 'WIO', 'NWC'), feature_group_count=x.shape[-1])
    return y + b.astype(x.dtype)


def rope_tables(s):
    rows = s // GRID_W
    row = jnp.repeat(jnp.arange(rows, dtype=jnp.float32), GRID_W)
    col = jnp.tile(jnp.arange(GRID_W, dtype=jnp.float32), rows)
    n_f = HEAD_DIM // 4
    inv = ROPE_THETA ** (-jnp.arange(n_f, dtype=jnp.float32) / n_f)
    ang = jnp.concatenate([row[:, None] * inv, col[:, None] * inv], axis=-1)
    return jnp.cos(ang), jnp.sin(ang)


def apply_rope(x, cos, sin):
    b, s, h, d = x.shape
    xp = x.reshape(b, s, h, d // 2, 2)
    x0 = xp[..., 0]
    x1 = xp[..., 1]
    cs = cos[None, :, None, :].astype(x.dtype)
    sn = sin[None, :, None, :].astype(x.dtype)
    out = jnp.stack([x0 * cs - x1 * sn, x0 * sn + x1 * cs], axis=-1)
    return out.reshape(b, s, h, d)


def attend(q, k, v):
    b, sq, h, d = q.shape
    nb = sq // Q_BLOCK
    qb = q.reshape(b, nb, Q_BLOCK, N_KV_HEADS, Q_PER_KV, d)
    qb = jnp.moveaxis(qb, 1, 0)

    def one_block(qblk):
        s = jnp.einsum('bqkgd,bskd->bkgqs', qblk, k).astype(jnp.float32) * ATTN_SCALE
        p = jax.nn.softmax(s, axis=-1).astype(v.dtype)
        return jnp.einsum('bkgqs,bskd->bqkgd', p, v)

    o = lax.map(one_block, qb)
    return jnp.moveaxis(o, 0, 1).reshape(b, sq, h * d)


def spatial_gate(u, v, g, w_s, b_s):
    b, s, _ = u.shape
    nc = s // CHUNK
    vn = rms_norm(v, g).reshape(b, nc, CHUNK, SGU_HEADS, SGU_HEAD_W)
    mixed = jnp.einsum('hnm,bcmhd->bcnhd', w_s.astype(vn.dtype), vn)
    mixed = mixed + b_s.T[None, None, :, :, None].astype(vn.dtype)
    return u * mixed.reshape(b, s, SGU_W)


def fourier_mix(f):
    b, s, _ = f.shape
    fg = f.reshape(b, s, FNET_GROUPS, FNET_GROUP_W).astype(jnp.float32)
    out = jnp.fft.fft2(fg, axes=(1, 3), norm='ortho').real
    return out.reshape(b, s, FNET_W).astype(f.dtype)


def token_mixer(h, p, ctx_k, ctx_v, latent):
    b, s, _ = h.shape
    z = h @ p['w_in']
    q, k, v, cv, sg, ff = jnp.split(z, SPLITS, axis=-1)
    q = rms_norm(q.reshape(b, s, N_Q_HEADS, HEAD_DIM), p['g_q'])
    k = rms_norm(k.reshape(b, s, N_KV_HEADS, HEAD_DIM), p['g_k'])
    v = v.reshape(b, s, N_KV_HEADS, HEAD_DIM)
    if latent:
        cos, sin = rope_tables(s)
        q = apply_rope(q, cos, sin)
        keys = jnp.concatenate([apply_rope(k, cos, sin), ctx_k.astype(k.dtype)], axis=1)
        vals = jnp.concatenate([v, ctx_v.astype(v.dtype)], axis=1)
    else:
        keys, vals = k, v
    attn = attend(q, keys, vals)
    a, gt = jnp.split(cv, 2, axis=-1)
    conv = a * jax.nn.sigmoid(gt)
    conv = dwconv(conv, p['conv_w'], p['conv_b'])
    conv = jax.nn.silu(layer_norm(conv, p['conv_ln_g'], p['conv_ln_b']))
    u, vv = jnp.split(sg, 2, axis=-1)
    sgu = spatial_gate(u, vv, p['sgu_g'], p['w_s'], p['b_s'])
    fnet = fourier_mix(ff)
    out = jnp.concatenate([attn, conv, sgu, fnet], axis=-1) @ p['w_out']
    return out, k, v


def conv_ffn(h, p):
    up = dwconv(h @ p['w_up'], p['ffn_conv_w'], p['ffn_conv_b'])
    a, bb = jnp.split(up, 2, axis=-1)
    return (jax.nn.silu(a) * bb) @ p['w_down']


def block(x, cond, p, ctx_k, ctx_v, latent):
    mod = jax.nn.silu(cond) @ p['w_ada'] + p['b_ada']
    sh1, sc1, gt1, sh2, sc2, gt2 = jnp.split(mod[:, None, :], 6, axis=-1)
    h = rms_norm(x, p['g_norm'][0]) * (1.0 + sc1) + sh1
    mix, k, v = token_mixer(h, p, ctx_k, ctx_v, latent)
    x = x + gt1 * rms_norm(mix, p['g_norm'][1])
    h = rms_norm(x, p['g_norm'][2]) * (1.0 + sc2) + sh2
    x = x + gt2 * rms_norm(conv_ffn(h, p), p['g_norm'][3])
    return x, k, v


def setup_inputs(seed: int = 0) -> dict:
    key = jax.random.key(seed)
    ks = jax.random.split(key, 24)
    f32 = jnp.float32
    nrm = lambda k, shape, s: jax.random.normal(k, shape, f32) * s
    return {
        'x_prompt': nrm(ks[0], (BATCH, SEQ, D_MODEL), 1.0),
        'x_sample': nrm(ks[1], (DEC_BATCH, DEC_SEQ, D_MODEL), 1.0),
        'cache_k': nrm(ks[2], (DEC_BATCH, DEPTH, PAST_LEN, N_KV_HEADS, HEAD_DIM), 1.0),
        'cache_v': nrm(ks[3], (DEC_BATCH, DEPTH, PAST_LEN, N_KV_HEADS, HEAD_DIM), 1.0),
        'c': nrm(ks[4], (DEC_BATCH, D_MODEL), 1.0),
        'c_ctx': nrm(ks[5], (D_MODEL,), 1.0),
        'w_ada': nrm(ks[6], (DEPTH, D_MODEL, 6 * D_MODEL), D_MODEL ** -0.5),
        'b_ada': nrm(ks[7], (DEPTH, 6 * D_MODEL), 0.01),
        'g_norm': 1.0 + nrm(ks[8], (DEPTH, 4, D_MODEL), 0.02),
        'w_in': nrm(ks[9], (DEPTH, D_MODEL, IN_COLS), D_MODEL ** -0.5),
        'g_q': 1.0 + nrm(ks[10], (DEPTH, HEAD_DIM), 0.02),
        'g_k': 1.0 + nrm(ks[11], (DEPTH, HEAD_DIM), 0.02),
        'conv_w': nrm(ks[12], (DEPTH, CONV_K, CONV_W), CONV_K ** -0.5),
        'conv_b': nrm(ks[13], (DEPTH, CONV_W), 0.01),
        'conv_ln_g': 1.0 + nrm(ks[14], (DEPTH, CONV_W), 0.02),
        'conv_ln_b': nrm(ks[15], (DEPTH, CONV_W), 0.01),
        'sgu_g': 1.0 + nrm(ks[16], (DEPTH, SGU_W), 0.02),
        'w_s': nrm(ks[17], (DEPTH, SGU_HEADS, CHUNK, CHUNK), CHUNK ** -0.5),
        'b_s': nrm(ks[18], (DEPTH, SGU_HEADS, CHUNK), 0.01),
        'w_out': nrm(ks[19], (DEPTH, D_MIX, D_MODEL), D_MIX ** -0.5),
        'w_up': nrm(ks[20], (DEPTH, D_MODEL, 2 * D_FF), D_MODEL ** -0.5),
        'ffn_conv_w': nrm(ks[21], (DEPTH, FFN_CONV_K, 2 * D_FF), FFN_CONV_K ** -0.5),
        'ffn_conv_b': nrm(ks[22], (DEPTH, 2 * D_FF), 0.01),
        'w_down': nrm(ks[23], (DEPTH, D_FF, D_MODEL), D_FF ** -0.5),
    }


def reference(x_prompt, x_sample, cache_k, cache_v, c, c_ctx, w_ada, b_ada, g_norm, w_in,
              g_q, g_k, conv_w, conv_b, conv_ln_g, conv_ln_b, sgu_g, w_s, b_s, w_out,
              w_up, ffn_conv_w, ffn_conv_b, w_down):
    params = [dict(w_ada=w_ada[l], b_ada=b_ada[l], g_norm=g_norm[l], w_in=w_in[l],
                   g_q=g_q[l], g_k=g_k[l], conv_w=conv_w[l], conv_b=conv_b[l],
                   conv_ln_g=conv_ln_g[l], conv_ln_b=conv_ln_b[l], sgu_g=sgu_g[l],
                   w_s=w_s[l], b_s=b_s[l], w_out=w_out[l], w_up=w_up[l],
                   ffn_conv_w=ffn_conv_w[l], ffn_conv_b=ffn_conv_b[l], w_down=w_down[l])
              for l in range(DEPTH)]

    xp = x_prompt
    ctx_cond = jnp.broadcast_to(c_ctx, (x_prompt.shape[0], c_ctx.shape[0]))
    ks_list = []
    vs_list = []
    for l in range(DEPTH):
        xp, k_l, v_l = block(xp, ctx_cond, params[l], None, None, False)
        ks_list.append(k_l)
        vs_list.append(v_l)
    y_prompt = xp
    new_k = jnp.stack(ks_list, axis=1)
    new_v = jnp.stack(vs_list, axis=1)

    xs = x_sample
    for l in range(DEPTH):
        xs, _, _ = block(xs, c, params[l], cache_k[:, l], cache_v[:, l], True)
    y_sample = xs

    return (y_prompt, y_sample, new_k, new_v)
```

```python
import functools

import numpy as np
import jax
import jax.numpy as jnp
from jax import lax
from jax.experimental import pallas as pl
from jax.experimental.pallas import tpu as pltpu

D_MODEL = 1024
DEPTH = 2
GRID_W = 64
HEAD_DIM = 64
N_Q_HEADS = 4
N_KV_HEADS = 2
Q_PER_KV = N_Q_HEADS // N_KV_HEADS
ATTN_W = N_Q_HEADS * HEAD_DIM
KV_W = N_KV_HEADS * HEAD_DIM
ATTN_SCALE = HEAD_DIM ** -0.5
ROPE_THETA = 10000.0
CONV_W = 256
CONV_K = 31
CONV_PAD = 16
SGU_W = 256
SGU_HEADS = 4
SGU_HEAD_W = SGU_W // SGU_HEADS
CHUNK = 128
FNET_W = 256
FNET_GROUPS = 4
FNET_GROUP_W = FNET_W // FNET_GROUPS
D_FF = 2816
FFN_CONV_K = 3
EPS = 1e-6
SPLIT_Q = ATTN_W
SPLIT_K = SPLIT_Q + KV_W
SPLIT_V = SPLIT_K + KV_W
SPLIT_CONV = SPLIT_V + 2 * CONV_W
SPLIT_SGU = SPLIT_CONV + 2 * SGU_W
IN_COLS = SPLIT_SGU + FNET_W

SUBLANES = 8
FFN_CHUNK = 256
FFT_RADIX = 64
FFT_CBLK = 8
VMEM_LIMIT = 56 * 1024 * 1024

F32 = jnp.float32
BF16 = jnp.bfloat16


def _cparams(n_axes):
    return pltpu.CompilerParams(dimension_semantics=("arbitrary",) * n_axes,
                                vmem_limit_bytes=VMEM_LIMIT)


def _const_spec(shape):
    nd = len(shape)
    return pl.BlockSpec(shape, lambda *_: (0,) * nd, pipeline_mode=pl.Buffered(1))


def _dot(a, b):
    return jnp.dot(a, b, preferred_element_type=F32)


def _dot_nt(a, b):
    return lax.dot_general(a, b, (((1,), (1,)), ((), ())), preferred_element_type=F32)


def _sigmoid(x):
    return 1.0 / (1.0 + jnp.exp(-x))


def _rms(x, g):
    ms = jnp.mean(x * x, axis=-1, keepdims=True)
    return x * lax.rsqrt(ms + EPS) * g


def _head_rms(z, pmat, g):
    sq = z * z
    hi = sq.astype(BF16)
    lo = (sq - hi.astype(F32)).astype(BF16)
    ms = _dot(hi, pmat) + _dot(lo, pmat)
    return z * lax.rsqrt(ms + EPS) * g


def _rope(x, cos, sin_signed):
    w = x.shape[1]
    nxt = pltpu.roll(x, w - 1, 1)
    prv = pltpu.roll(x, 1, 1)
    lane = lax.broadcasted_iota(jnp.int32, x.shape, 1)
    swapped = jnp.where((lane & 1) == 0, nxt, prv)
    return x * cos + swapped * sin_signed


def _ada_kernel(cond_ref, w_ref, b_ref, o_ref):
    c = cond_ref[...]
    s = (c * _sigmoid(c)).astype(BF16)
    o_ref[0] = _dot(s, w_ref[0].astype(BF16)) + b_ref[0]


def _ada(cond8, w_ada, b_ada):
    tn = 1536
    n_out = 6 * D_MODEL
    return pl.pallas_call(
        _ada_kernel,
        grid=(DEPTH, n_out // tn),
        in_specs=[pl.BlockSpec((SUBLANES, D_MODEL), lambda l, j: (0, 0)),
                  pl.BlockSpec((1, D_MODEL, tn), lambda l, j: (l, 0, j)),
                  pl.BlockSpec((1, 1, tn), lambda l, j: (l, 0, j))],
        out_specs=pl.BlockSpec((1, SUBLANES, tn), lambda l, j: (l, 0, j)),
        out_shape=jax.ShapeDtypeStruct((DEPTH, SUBLANES, n_out), F32),
        compiler_params=_cparams(2),
        name="ada",
    )(cond8, w_ada, b_ada.reshape(DEPTH, 1, n_out))


def _front_kernel(*refs, latent):
    if latent:
        (x_ref, mod_ref, gn_ref, win_ref, gq_ref, gk_ref, pm_ref, sg_ref, cc_ref, cs_ref,
         cos_ref, sin_ref, q_ref, k_ref, v_ref, ci_ref, u_ref, vn_ref, gr_ref, gi_ref) = refs
    else:
        (x_ref, mod_ref, gn_ref, win_ref, gq_ref, gk_ref, pm_ref, sg_ref, cc_ref, cs_ref,
         q_ref, k_ref, v_ref, ci_ref, u_ref, vn_ref, gr_ref, gi_ref) = refs
    m = mod_ref[0]
    sh1, sc1 = m[0:1], m[1:2]
    h = _rms(x_ref[0], gn_ref[0:1]) * (1.0 + sc1) + sh1
    z = _dot(h.astype(BF16), win_ref[...])
    pm = pm_ref[...]
    q = _head_rms(z[:, :SPLIT_Q], pm, gq_ref[...])
    k = _head_rms(z[:, SPLIT_Q:SPLIT_K], pm[:KV_W, :KV_W], gk_ref[...])
    v = z[:, SPLIT_K:SPLIT_V]
    if latent:
        cos = cos_ref[...]
        sin = sin_ref[...]
        q = _rope(q, cos, sin)
        k = _rope(k, cos[:, :KV_W], sin[:, :KV_W])
    q_ref[0] = q.astype(q_ref.dtype)
    k_ref[0] = k.astype(k_ref.dtype)
    v_ref[0] = v.astype(v_ref.dtype)
    a = z[:, SPLIT_V:SPLIT_V + CONV_W]
    gt = z[:, SPLIT_V + CONV_W:SPLIT_CONV]
    ci_ref[0] = a * _sigmoid(gt)
    u_ref[0] = z[:, SPLIT_CONV:SPLIT_CONV + SGU_W]
    vn_ref[0] = _rms(z[:, SPLIT_CONV + SGU_W:SPLIT_SGU], sg_ref[...]).astype(vn_ref.dtype)
    ff = z[:, SPLIT_SGU:].astype(BF16)
    gr_ref[0] = _dot(ff, cc_ref[...]).astype(gr_ref.dtype)
    gi_ref[0] = (-_dot(ff, cs_ref[...])).astype(gi_ref.dtype)


def _front(x, mod, gn, w_in, gq, gk, pm, sgu_g, cc, cs, rope, *, latent, tm):
    b, s, _ = x.shape
    nmod = mod.shape[0]
    row = lambda w: pl.BlockSpec((1, tm, w), lambda i, t: (i, t, 0))
    in_specs = [row(D_MODEL),
                pl.BlockSpec((1, 6, D_MODEL), lambda i, t: (i % nmod, 0, 0)),
                _const_spec((4, D_MODEL)),
                _const_spec((D_MODEL, IN_COLS)),
                _const_spec((1, ATTN_W)), _const_spec((1, KV_W)),
                _const_spec((ATTN_W, ATTN_W)), _const_spec((1, SGU_W)),
                _const_spec((FNET_W, FNET_W)), _const_spec((FNET_W, FNET_W))]
    args = [x, mod, gn, w_in, gq, gk, pm, sgu_g, cc, cs]
    if latent:
        in_specs += [pl.BlockSpec((tm, ATTN_W), lambda i, t: (t, 0))] * 2
        args += list(rope)
    kv_dt = BF16 if latent else F32
    outs = [(ATTN_W, BF16), (KV_W, kv_dt), (KV_W, kv_dt), (CONV_W, F32), (SGU_W, F32),
            (SGU_W, BF16), (FNET_W, BF16), (FNET_W, BF16)]
    return pl.pallas_call(
        functools.partial(_front_kernel, latent=latent),
        grid=(b, s // tm),
        in_specs=in_specs,
        out_specs=[row(w) for w, _ in outs],
        out_shape=[jax.ShapeDtypeStruct((b, s, w), dt) for w, dt in outs],
        compiler_params=_cparams(2),
        name="front_lat" if latent else "front_ctx",
    )(*args)


def _fft1_kernel(gr_ref, gi_ref, m1_ref, yr_ref, yi_ref):
    cat = jnp.concatenate([gr_ref[0], gi_ref[0]], axis=0)
    y = _dot(m1_ref[...], cat)
    yr_ref[0] = y[:FFT_RADIX].astype(yr_ref.dtype)
    yi_ref[0] = y[FFT_RADIX:].astype(yi_ref.dtype)


def _fft1(gr, gi, m1):
    b = gr.shape[0]
    n = gr.shape[2]
    tn = 2048
    blk = pl.BlockSpec((1, FFT_RADIX, tn), lambda i, j: (i, 0, j))
    return pl.pallas_call(
        _fft1_kernel,
        grid=(b, n // tn),
        in_specs=[blk, blk, _const_spec((2 * FFT_RADIX, 2 * FFT_RADIX))],
        out_specs=[blk, blk],
        out_shape=[jax.ShapeDtypeStruct(gr.shape, BF16)] * 2,
        compiler_params=_cparams(2),
        name="fft1",
    )(gr, gi, m1)


def _fft2_kernel(yr_ref, yi_ref, g_ref, o_ref):
    cat = jnp.concatenate([yr_ref[0], yi_ref[0]], axis=0)
    y = _dot(g_ref[0], cat)
    o_ref[0] = y.reshape(FFT_RADIX, FFT_CBLK, FNET_W)


def _fft2(yr, yi, g2):
    b = yr.shape[0]
    rows = FFT_RADIX * FFT_CBLK
    nblk = FFT_RADIX // FFT_CBLK
    yblk = pl.BlockSpec((1, rows, FNET_W), lambda j, i: (i, j, 0))
    return pl.pallas_call(
        _fft2_kernel,
        grid=(nblk, b),
        in_specs=[yblk, yblk, pl.BlockSpec((1, rows, 2 * rows), lambda j, i: (j, 0, 0))],
        out_specs=pl.BlockSpec((1, FFT_RADIX, FFT_CBLK, FNET_W), lambda j, i: (i, 0, j, 0)),
        out_shape=jax.ShapeDtypeStruct((b, FFT_RADIX, FFT_RADIX, FNET_W), F32),
        compiler_params=_cparams(2),
        name="fft2",
    )(yr, yi, g2)


def _attend_head(qh, parts):
    scores = [_dot_nt(qh, kk) * ATTN_SCALE for kk, _ in parts]
    mx = functools.reduce(jnp.maximum, [jnp.max(s, axis=-1, keepdims=True) for s in scores])
    den = 0.0
    out = 0.0
    for s, (_, vv) in zip(scores, parts):
        p = jnp.exp(s - mx)
        den = den + jnp.sum(p, axis=-1, keepdims=True)
        out = out + _dot(p.astype(BF16), vv)
    return out / den


def _back_kernel(*refs, latent, tq):
    if latent:
        (x_ref, mod_ref, gn_ref, q_ref, k_ref, v_ref, ck_ref, cv_ref, ci_ref, cw_ref, cb_ref,
         lg_ref, lb_ref, u_ref, vn_ref, ws_ref, bs_ref, fn_ref, wo_ref, o_ref) = refs
    else:
        (x_ref, mod_ref, gn_ref, q_ref, k_ref, v_ref, ci_ref, cw_ref, cb_ref,
         lg_ref, lb_ref, u_ref, vn_ref, ws_ref, bs_ref, gr_ref, gi_ref, c256_ref, s256_ref,
         wo_ref, o_ref) = refs
    t = pl.program_id(1)

    q = q_ref[0]
    k = k_ref[0].astype(BF16)
    v = v_ref[0].astype(BF16)
    if latent:
        ck = ck_ref[0].astype(BF16)
        cv = cv_ref[0].astype(BF16)
    heads = []
    for hd in range(N_Q_HEADS):
        g = hd // Q_PER_KV
        ksl = slice(g * HEAD_DIM, (g + 1) * HEAD_DIM)
        parts = [(k[:, ksl], v[:, ksl])]
        if latent:
            parts.append((ck[:, ksl], cv[:, ksl]))
        heads.append(_attend_head(q[:, hd * HEAD_DIM:(hd + 1) * HEAD_DIM], parts))
    attn = jnp.concatenate(heads, axis=-1)

    win_rows = tq + 2 * CONV_PAD
    win = ci_ref[0, pl.ds(pl.multiple_of(t * tq, SUBLANES), win_rows), :]
    shifted = [win] + [pltpu.roll(win, win_rows - sft, 0) for sft in range(1, SUBLANES)]
    cw = cw_ref[...]
    conv = jnp.zeros((tq, CONV_W), F32) + cb_ref[...]
    for j in range(CONV_K):
        off = j + CONV_PAD - CONV_K // 2
        base = (off // SUBLANES) * SUBLANES
        conv = conv + cw[j:j + 1] * shifted[off % SUBLANES][base:base + tq]
    mu = jnp.mean(conv, axis=-1, keepdims=True)
    cen = conv - mu
    var = jnp.mean(cen * cen, axis=-1, keepdims=True)
    conv = cen * lax.rsqrt(var + EPS) * lg_ref[...] + lb_ref[...]
    conv = conv * _sigmoid(conv)

    lane = lax.broadcasted_iota(jnp.int32, (CHUNK, SGU_W), 1)
    mixed_chunks = []
    for c in range(tq // CHUNK):
        vn_c = vn_ref[0, c * CHUNK:(c + 1) * CHUNK, :]
        mixed = jnp.zeros((CHUNK, SGU_W), F32)
        for hd in range(SGU_HEADS):
            full = _dot(ws_ref[hd], vn_c)
            in_head = (lane >= hd * SGU_HEAD_W) & (lane < (hd + 1) * SGU_HEAD_W)
            mixed = jnp.where(in_head, full, mixed)
        mixed_chunks.append(mixed + bs_ref[...])
    sgu = u_ref[0] * jnp.concatenate(mixed_chunks, axis=0)

    if latent:
        fnet = fn_ref[0]
    else:
        fnet = (_dot(c256_ref[...], gr_ref[0]) + _dot(s256_ref[...], gi_ref[0])) * (
            1.0 / np.sqrt(tq * FNET_GROUP_W))

    mix = (_dot(attn.astype(BF16), wo_ref[0:ATTN_W, :])
           + _dot(conv.astype(BF16), wo_ref[ATTN_W:ATTN_W + CONV_W, :])
           + _dot(sgu.astype(BF16), wo_ref[ATTN_W + CONV_W:ATTN_W + CONV_W + SGU_W, :])
           + _dot(fnet.astype(BF16), wo_ref[ATTN_W + CONV_W + SGU_W:, :]))
    gt1 = mod_ref[0][2:3]
    o_ref[0] = x_ref[0] + gt1 * _rms(mix, gn_ref[1:2])


def _back(x, mod, gn, q, k, v, cache, ci_pad, cw, cb, lg, lb, u, vn, ws, bs, fn_args, wo,
          *, latent, tq):
    b, s, _ = x.shape
    nmod = mod.shape[0]
    row = lambda w: pl.BlockSpec((1, tq, w), lambda i, t: (i, t, 0))
    per_batch = lambda r, w: pl.BlockSpec((1, r, w), lambda i, t: (i, 0, 0))
    in_specs = [row(D_MODEL),
                pl.BlockSpec((1, 6, D_MODEL), lambda i, t: (i % nmod, 0, 0)),
                _const_spec((4, D_MODEL)),
                row(ATTN_W), per_batch(s, KV_W), per_batch(s, KV_W)]
    args = [x, mod, gn, q, k, v]
    if latent:
        past = cache[0].shape[1]
        in_specs += [per_batch(past, KV_W)] * 2
        args += list(cache)
    in_specs += [per_batch(s + 2 * CONV_PAD, CONV_W),
                 _const_spec(cw.shape), _const_spec((1, CONV_W)),
                 _const_spec((1, CONV_W)), _const_spec((1, CONV_W)),
                 row(SGU_W), row(SGU_W),
                 _const_spec((SGU_HEADS, CHUNK, CHUNK)), _const_spec((CHUNK, SGU_W))]
    args += [ci_pad, cw, cb, lg, lb, u, vn, ws, bs]
    if latent:
        in_specs += [row(FNET_W)]
    else:
        in_specs += [per_batch(s, FNET_W), per_batch(s, FNET_W),
                     _const_spec((s, s)), _const_spec((s, s))]
    args += list(fn_args)
    in_specs += [_const_spec((D_MODEL, D_MODEL))]
    args += [wo]
    return pl.pallas_call(
        functools.partial(_back_kernel, latent=latent, tq=tq),
        grid=(b, s // tq),
        in_specs=in_specs,
        out_specs=row(D_MODEL),
        out_shape=jax.ShapeDtypeStruct(x.shape, F32),
        compiler_params=_cparams(2),
        name="back_lat" if latent else "back_ctx",
    )(*args)


def _ffn_kernel(x_ref, xp_ref, xn_ref, mod_ref, gn_ref, wup_ref, cw_ref, cb_ref, wdn_ref, o_ref,
                *, tm, nt):
    t = pl.program_id(1)
    m = mod_ref[0]
    sh2, sc2, gt2 = m[3:4], m[4:5], m[5:6]
    x = x_ref[0]
    ext = tm + 2 * SUBLANES
    xe = jnp.concatenate([xp_ref[0], x, xn_ref[0]], axis=0)
    he = _rms(xe, gn_ref[2:3]) * (1.0 + sc2) + sh2
    row = lax.broadcasted_iota(jnp.int32, (ext, D_MODEL), 0)
    lo = jnp.where(t > 0, 0, SUBLANES)
    hi = jnp.where(t < nt - 1, ext, tm + SUBLANES)
    hb = jnp.where((row >= lo) & (row < hi), he, 0.0).astype(BF16)

    def conv3(up, col0):
        w = cw_ref[:, col0:col0 + FFN_CHUNK]
        prev = pltpu.roll(up, 1, 0)[SUBLANES:SUBLANES + tm]
        nxt = pltpu.roll(up, ext - 1, 0)[SUBLANES:SUBLANES + tm]
        mid = up[SUBLANES:SUBLANES + tm]
        return w[0:1] * prev + w[1:2] * mid + w[2:3] * nxt + cb_ref[:, col0:col0 + FFN_CHUNK]

    acc = jnp.zeros((tm, D_MODEL), F32)
    for c in range(D_FF // FFN_CHUNK):
        ca = c * FFN_CHUNK
        cg = D_FF + c * FFN_CHUNK
        a = conv3(_dot(hb, wup_ref[:, ca:ca + FFN_CHUNK]), ca)
        g = conv3(_dot(hb, wup_ref[:, cg:cg + FFN_CHUNK]), cg)
        act = a * _sigmoid(a) * g
        acc = acc + _dot(act.astype(BF16), wdn_ref[ca:ca + FFN_CHUNK, :])
    o_ref[0] = x + gt2 * _rms(acc, gn_ref[3:4])


def _ffn(x, mod, gn, w_up, cw, cb, w_dn, *, tm):
    b, s, _ = x.shape
    nmod = mod.shape[0]
    nt = s // tm
    per8 = tm // SUBLANES
    last8 = s // SUBLANES - 1
    return pl.pallas_call(
        functools.partial(_ffn_kernel, tm=tm, nt=nt),
        grid=(b, nt),
        in_specs=[pl.BlockSpec((1, tm, D_MODEL), lambda i, t: (i, t, 0)),
                  pl.BlockSpec((1, SUBLANES, D_MODEL),
                               lambda i, t: (i, jnp.maximum(t * per8 - 1, 0), 0)),
                  pl.BlockSpec((1, SUBLANES, D_MODEL),
                               lambda i, t: (i, jnp.minimum((t + 1) * per8, last8), 0)),
                  pl.BlockSpec((1, 6, D_MODEL), lambda i, t: (i % nmod, 0, 0)),
                  _const_spec((4, D_MODEL)),
                  _const_spec((D_MODEL, 2 * D_FF)),
                  _const_spec((SUBLANES, 2 * D_FF)), _const_spec((1, 2 * D_FF)),
                  _const_spec((D_FF, D_MODEL))],
        out_specs=pl.BlockSpec((1, tm, D_MODEL), lambda i, t: (i, t, 0)),
        out_shape=jax.ShapeDtypeStruct(x.shape, F32),
        compiler_params=_cparams(2),
        name="ffn",
    )(x, x, x, mod, gn, w_up, cw, cb, w_dn)


def _np_consts(ctx_seq):
    hd = np.arange(ATTN_W) // HEAD_DIM
    pmat = (hd[:, None] == hd[None, :]).astype(np.float32) / HEAD_DIM
    ch = np.arange(FNET_W)
    same = (ch[:, None] // FNET_GROUP_W) == (ch[None, :] // FNET_GROUP_W)
    ang = 2.0 * np.pi * ((ch[:, None] % FNET_GROUP_W) * (ch[None, :] % FNET_GROUP_W)) / FNET_GROUP_W
    cc = np.where(same, np.cos(ang), 0.0).astype(np.float32)
    cs = np.where(same, np.sin(ang), 0.0).astype(np.float32)
    n = np.arange(ctx_seq)
    ang = 2.0 * np.pi * ((n[:, None] * n[None, :]) % ctx_seq) / ctx_seq
    c_ctx = np.cos(ang).astype(np.float32)
    s_ctx = np.sin(ang).astype(np.float32)
    r = np.arange(FFT_RADIX)
    ang = 2.0 * np.pi * ((r[:, None] * r[None, :]) % FFT_RADIX) / FFT_RADIX
    c64, s64 = np.cos(ang), np.sin(ang)
    m1 = np.block([[c64, s64], [-s64, c64]]).astype(np.float32)
    n_seq = FFT_RADIX * FFT_RADIX
    d = r[:, None, None]
    c = r[None, :, None]
    bb = r[None, None, :]
    theta = 2.0 * np.pi * ((bb * (c + FFT_RADIX * d)) % n_seq) / n_seq
    scale = 1.0 / np.sqrt(n_seq * FNET_GROUP_W)
    tw = np.stack([np.cos(theta), np.sin(theta)], axis=0) * scale
    return pmat, cc, cs, c_ctx, s_ctx, m1, tw.astype(np.float32)


def _stage2_matrix(tw):
    nblk = FFT_RADIX // FFT_CBLK
    t5 = tw.reshape(2, FFT_RADIX, nblk, FFT_CBLK, FFT_RADIX)
    eye = jnp.eye(FFT_CBLK, dtype=F32)
    g = t5[:, :, :, :, None, :] * eye[None, None, None, :, :, None]
    g = jnp.transpose(g, (2, 1, 3, 0, 4, 5))
    return g.reshape(nblk, FFT_RADIX * FFT_CBLK, 2 * FFT_CBLK * FFT_RADIX).astype(BF16)


def _rope_tables(s):
    rows = s // GRID_W
    row = jnp.repeat(jnp.arange(rows, dtype=F32), GRID_W)
    col = jnp.tile(jnp.arange(GRID_W, dtype=F32), rows)
    n_f = HEAD_DIM // 4
    inv = ROPE_THETA ** (-jnp.arange(n_f, dtype=F32) / n_f)
    ang = jnp.concatenate([row[:, None] * inv, col[:, None] * inv], axis=-1)
    cos = jnp.repeat(jnp.cos(ang), 2, axis=-1)
    sin = jnp.repeat(jnp.sin(ang), 2, axis=-1)
    sign = jnp.tile(jnp.array([-1.0, 1.0], F32), HEAD_DIM // 2)
    return jnp.tile(cos, (1, N_Q_HEADS)), jnp.tile(sin * sign, (1, N_Q_HEADS))


def kernel(x_prompt, x_sample, cache_k, cache_v, c, c_ctx, w_ada, b_ada, g_norm, w_in, g_q, g_k, conv_w, conv_b, conv_ln_g, conv_ln_b, sgu_g, w_s, b_s, w_out, w_up, ffn_conv_w, ffn_conv_b, w_down):
    bsz, seq, _ = x_prompt.shape
    dec_b, dec_s, _ = x_sample.shape
    assert dec_s == FFT_RADIX * FFT_RADIX and seq % CHUNK == 0

    pmat, cc, cs, c_ctx_m, s_ctx_m, m1, tw = _np_consts(seq)
    pmat, cc, cs, c_ctx_m, s_ctx_m, m1 = (jnp.asarray(a).astype(BF16)
                                           for a in (pmat, cc, cs, c_ctx_m, s_ctx_m, m1))
    g2 = _stage2_matrix(jnp.asarray(tw))
    rope = _rope_tables(dec_s)

    cond8 = jnp.zeros((SUBLANES, D_MODEL), F32).at[0].set(c_ctx).at[1:1 + dec_b].set(c)
    mod = _ada(cond8, w_ada, b_ada).reshape(DEPTH, SUBLANES, 6, D_MODEL)

    w_in_b, w_out_b, w_up_b, w_dn_b = (w.astype(BF16) for w in (w_in, w_out, w_up, w_down))
    ws_b = w_s.astype(BF16)
    cw_pad = jnp.pad(conv_w, ((0, 0), (0, 32 - CONV_K), (0, 0)))
    fcw_pad = jnp.pad(ffn_conv_w, ((0, 0), (0, SUBLANES - FFN_CONV_K), (0, 0)))
    bs_full = jnp.repeat(jnp.swapaxes(b_s, 1, 2), SGU_HEAD_W, axis=2)
    row1 = lambda a, l: a[l][None, :]

    def layer(x, l, mod_l, latent, tm_front, tq, tm_ffn):
        b, s, _ = x.shape
        gq = jnp.tile(g_q[l], N_Q_HEADS)[None, :]
        gk = jnp.tile(g_k[l], N_KV_HEADS)[None, :]
        q, k, v, ci, u, vn, gr, gi = _front(
            x, mod_l, g_norm[l], w_in_b[l], gq, gk, pmat, row1(sgu_g, l), cc, cs,
            rope if latent else None, latent=latent, tm=tm_front)
        ci_pad = jnp.pad(ci, ((0, 0), (CONV_PAD, CONV_PAD), (0, 0)))
        if latent:
            yr, yi = _fft1(gr.reshape(b, FFT_RADIX, -1), gi.reshape(b, FFT_RADIX, -1), m1)
            fn = _fft2(yr.reshape(b, s, FNET_W), yi.reshape(b, s, FNET_W), g2)
            fn_args = [fn.reshape(b, s, FNET_W)]
            cache = [cache_k[:, l].reshape(b, -1, KV_W), cache_v[:, l].reshape(b, -1, KV_W)]
        else:
            fn_args = [gr, gi, c_ctx_m, s_ctx_m]
            cache = None
        x = _back(x, mod_l, g_norm[l], q, k, v, cache, ci_pad, cw_pad[l], row1(conv_b, l),
                  row1(conv_ln_g, l), row1(conv_ln_b, l), u, vn, ws_b[l], bs_full[l],
                  fn_args, w_out_b[l], latent=latent, tq=tq)
        x = _ffn(x, mod_l, g_norm[l], w_up_b[l], fcw_pad[l], row1(ffn_conv_b, l), w_dn_b[l],
                 tm=tm_ffn)
        return x, k, v

    xp = x_prompt
    ks, vs = [], []
    for l in range(DEPTH):
        xp, k_l, v_l = layer(xp, l, mod[l, 0:1], False, seq, seq, seq)
        ks.append(k_l.reshape(bsz, seq, N_KV_HEADS, HEAD_DIM))
        vs.append(v_l.reshape(bsz, seq, N_KV_HEADS, HEAD_DIM))
    new_k = jnp.stack(ks, axis=1)
    new_v = jnp.stack(vs, axis=1)

    xs = x_sample
    for l in range(DEPTH):
        xs, _, _ = layer(xs, l, mod[l, 1:1 + dec_b], True, 512, 256, 512)

    return (xp, xs, new_k, new_v)
```

```python
import functools

import numpy as np
import jax
import jax.numpy as jnp
from jax import lax
from jax.experimental import pallas as pl
from jax.experimental.pallas import tpu as pltpu

D_MODEL = 1024
DEPTH = 2
GRID_W = 64
HEAD_DIM = 64
N_Q_HEADS = 4
N_KV_HEADS = 2
Q_PER_KV = N_Q_HEADS // N_KV_HEADS
ATTN_W = N_Q_HEADS * HEAD_DIM
KV_W = N_KV_HEADS * HEAD_DIM
ATTN_SCALE = HEAD_DIM ** -0.5
ROPE_THETA = 10000.0
CONV_W = 256
CONV_K = 31
CONV_PAD = 16
SGU_W = 256
SGU_HEADS = 4
SGU_HEAD_W = SGU_W // SGU_HEADS
CHUNK = 128
FNET_W = 256
FNET_GROUPS = 4
FNET_GROUP_W = FNET_W // FNET_GROUPS
D_FF = 2816
FFN_CONV_K = 3
EPS = 1e-6
SPLIT_Q = ATTN_W
SPLIT_K = SPLIT_Q + KV_W
SPLIT_V = SPLIT_K + KV_W
SPLIT_CONV = SPLIT_V + 2 * CONV_W
SPLIT_SGU = SPLIT_CONV + 2 * SGU_W
IN_COLS = SPLIT_SGU + FNET_W

SUBLANES = 8
LANES = 128
ROW_TILE = 512
LAT_TQ = 256
FFN_CHUNK = 256
FFT_RADIX = 64
FFT_BLK = 8
VMEM_LIMIT = 56 * 1024 * 1024

F32 = jnp.float32
BF16 = jnp.bfloat16


def _cparams(n_axes):
    return pltpu.CompilerParams(dimension_semantics=("arbitrary",) * n_axes,
                                vmem_limit_bytes=VMEM_LIMIT)


def _const_spec(shape):
    nd = len(shape)
    return pl.BlockSpec(shape, lambda *_: (0,) * nd, pipeline_mode=pl.Buffered(1))


def _dot(a, b):
    return jnp.dot(a, b, preferred_element_type=F32)


def _dot_nt(a, b):
    return lax.dot_general(a, b, (((1,), (1,)), ((), ())), preferred_element_type=F32)


def _sigmoid(x):
    return 1.0 / (1.0 + jnp.exp(-x))


def _rms(x, g):
    ms = jnp.mean(x * x, axis=-1, keepdims=True)
    return x * lax.rsqrt(ms + EPS) * g


def _head_rms(z, pmat, g):
    sq = z * z
    hi = sq.astype(BF16)
    lo = (sq - hi.astype(F32)).astype(BF16)
    ms = _dot(hi, pmat) + _dot(lo, pmat)
    return z * lax.rsqrt(ms + EPS) * g


def _rope(x, cos, sin_signed):
    w = x.shape[1]
    nxt = pltpu.roll(x, w - 1, 1)
    prv = pltpu.roll(x, 1, 1)
    lane = lax.broadcasted_iota(jnp.int32, x.shape, 1)
    swapped = jnp.where((lane & 1) == 0, nxt, prv)
    return x * cos + swapped * sin_signed


def _ada_kernel(cond_ref, w_ref, b_ref, o_ref):
    c = cond_ref[...]
    s = (c * _sigmoid(c)).astype(BF16)
    o_ref[0] = _dot(s, w_ref[0].astype(BF16)) + b_ref[0]


def _ada(cond8, w_ada, b_ada):
    tn = 1536
    n_out = 6 * D_MODEL
    return pl.pallas_call(
        _ada_kernel,
        grid=(DEPTH, n_out // tn),
        in_specs=[pl.BlockSpec((SUBLANES, D_MODEL), lambda l, j: (0, 0)),
                  pl.BlockSpec((1, D_MODEL, tn), lambda l, j: (l, 0, j)),
                  pl.BlockSpec((1, 1, tn), lambda l, j: (l, 0, j))],
        out_specs=pl.BlockSpec((1, SUBLANES, tn), lambda l, j: (l, 0, j)),
        out_shape=jax.ShapeDtypeStruct((DEPTH, SUBLANES, n_out), F32),
        compiler_params=_cparams(2),
        name="ada",
    )(cond8, w_ada, b_ada.reshape(DEPTH, 1, n_out))


def _front_kernel(*refs, latent, seq):
    (x_ref, mod_ref, gn_ref, win_ref, gq_ref, gk_ref, pm_ref, sg_ref, cc_ref, cs_ref) = refs[:10]
    if latent:
        cos_ref, sin_ref = refs[10:12]
        q_ref, k_ref, v_ref, ci_ref, u_ref, vn_ref, gr_ref, gi_ref = refs[12:]
    else:
        q_ref, k_ref, v_ref, ci_ref, u_ref, vn_ref, gr_ref, gi_ref = refs[10:]
    m = mod_ref[0]
    sh1, sc1 = m[0:1], m[1:2]
    h = _rms(x_ref[...], gn_ref[0:1]) * (1.0 + sc1) + sh1
    z = _dot(h.astype(BF16), win_ref[...])
    pm = pm_ref[...]
    q = _head_rms(z[:, :SPLIT_Q], pm, gq_ref[...])
    k = _head_rms(z[:, SPLIT_Q:SPLIT_K], pm[:KV_W, :KV_W], gk_ref[...])
    v = z[:, SPLIT_K:SPLIT_V]
    if latent:
        cos = cos_ref[...]
        sin = sin_ref[...]
        q = _rope(q, cos, sin)
        k = _rope(k, cos[:, :KV_W], sin[:, :KV_W])
        k_ref[...] = k.astype(k_ref.dtype)
        v_ref[...] = v.astype(v_ref.dtype)
    else:
        rows = x_ref.shape[0]
        k_ref[:, 0] = k.reshape(rows // seq, seq, KV_W)
        v_ref[:, 0] = v.reshape(rows // seq, seq, KV_W)
    q_ref[...] = (q * ATTN_SCALE).astype(q_ref.dtype)
    a = z[:, SPLIT_V:SPLIT_V + CONV_W]
    gt = z[:, SPLIT_V + CONV_W:SPLIT_CONV]
    ci_ref[...] = a * _sigmoid(gt)
    u_ref[...] = z[:, SPLIT_CONV:SPLIT_CONV + SGU_W]
    vn_ref[...] = _rms(z[:, SPLIT_CONV + SGU_W:SPLIT_SGU], sg_ref[...]).astype(vn_ref.dtype)
    ff = z[:, SPLIT_SGU:].astype(BF16)
    gr_ref[...] = _dot(ff, cc_ref[...]).astype(gr_ref.dtype)
    gi_ref[...] = (-_dot(ff, cs_ref[...])).astype(gi_ref.dtype)


def _front(x, mod, gn, w_in, gq, gk, pm, sgu_g, cc, cs, rope, kv_prev, *, latent, seq, layer):
    n = x.shape[0]
    tm = ROW_TILE
    nt = n // tm
    nmod = mod.shape[0]
    nb = n // seq
    row = lambda w: pl.BlockSpec((tm, w), lambda t: (t, 0))
    in_specs = [row(D_MODEL),
                pl.BlockSpec((1, 6, D_MODEL), lambda t: ((t * nmod) // nt, 0, 0)),
                _const_spec((4, D_MODEL)),
                _const_spec((D_MODEL, IN_COLS)),
                _const_spec((1, ATTN_W)), _const_spec((1, KV_W)),
                _const_spec((ATTN_W, ATTN_W)), _const_spec((1, SGU_W)),
                _const_spec((FNET_W, FNET_W)), _const_spec((FNET_W, FNET_W))]
    args = [x, mod, gn, w_in, gq, gk, pm, sgu_g, cc, cs]
    aliases = {}
    if latent:
        per_seq = seq // tm
        in_specs += [pl.BlockSpec((tm, ATTN_W), lambda t: (t % per_seq, 0))] * 2
        args += list(rope)
        kv_specs = [row(KV_W), row(KV_W)]
        kv_shapes = [jax.ShapeDtypeStruct((n, KV_W), BF16)] * 2
        f_dt = F32
    else:
        per_tile = tm // seq
        cache_spec = pl.BlockSpec((per_tile, 1, seq, KV_W), lambda t: (t, layer, 0, 0))
        kv_specs = [cache_spec, cache_spec]
        kv_shapes = [jax.ShapeDtypeStruct((nb, DEPTH, seq, KV_W), F32)] * 2
        if kv_prev is not None:
            in_specs += [pl.BlockSpec(memory_space=pl.ANY)] * 2
            args += list(kv_prev)
            aliases = {len(args) - 2: 1, len(args) - 1: 2}
        f_dt = BF16
    outs = [(ATTN_W, BF16), None, None, (CONV_W, F32), (SGU_W, F32), (SGU_W, BF16),
            (FNET_W, f_dt), (FNET_W, f_dt)]
    out_specs = [row(o[0]) if o else None for o in outs]
    out_shape = [jax.ShapeDtypeStruct((n, o[0]), o[1]) if o else None for o in outs]
    out_specs[1:3] = kv_specs
    out_shape[1:3] = kv_shapes

    def body(*refs):
        if not latent and kv_prev is not None:
            refs = refs[:10] + refs[12:]
        _front_kernel(*refs, latent=latent, seq=seq)

    return pl.pallas_call(
        body,
        grid=(nt,),
        in_specs=in_specs,
        out_specs=out_specs,
        out_shape=out_shape,
        input_output_aliases=aliases,
        compiler_params=_cparams(1),
        name="front_lat" if latent else "front_ctx",
    )(*args)


def _fft_kernel(gr_ref, gi_ref, k1_ref, tc_ref, ts_ref, k2_ref, o_ref, yr_ref, yi_ref):
    j = pl.program_id(1)
    nblk = FFT_RADIX // FFT_BLK
    rows = FFT_RADIX * FFT_BLK

    @pl.when(j < nblk)
    def _():
        g = jnp.concatenate([gr_ref[0].reshape(rows, FNET_W), gi_ref[0].reshape(rows, FNET_W)],
                            axis=0).astype(BF16)
        y = _dot(k1_ref[...], g)
        yr, yi = y[:rows], y[rows:]
        tc = tc_ref[...].reshape(rows, LANES)
        ts = ts_ref[...].reshape(rows, LANES)
        tc = jnp.concatenate([tc] * (FNET_W // LANES), axis=1)
        ts = jnp.concatenate([ts] * (FNET_W // LANES), axis=1)
        col = pl.ds(pl.multiple_of(j * FFT_BLK, FFT_BLK), FFT_BLK)
        yr_ref[:, col, :] = (yr * tc + yi * ts).reshape(FFT_RADIX, FFT_BLK, FNET_W)
        yi_ref[:, col, :] = (yi * tc - yr * ts).reshape(FFT_RADIX, FFT_BLK, FNET_W)

    @pl.when(j >= nblk)
    def _():
        blk = pl.ds(pl.multiple_of((j - nblk) * FFT_BLK, FFT_BLK), FFT_BLK)
        y = jnp.concatenate([yr_ref[blk].reshape(rows, FNET_W), yi_ref[blk].reshape(rows, FNET_W)],
                            axis=0).astype(BF16)
        o_ref[0] = _dot(k2_ref[...], y).reshape(FFT_RADIX, FFT_BLK, FNET_W)


def _fft(gr, gi, k1, tc, ts, k2):
    nb = gr.shape[0]
    nblk = FFT_RADIX // FFT_BLK
    rows = FFT_RADIX * FFT_BLK
    g_spec = pl.BlockSpec((1, FFT_RADIX, FFT_BLK, FNET_W),
                          lambda i, j: (i, 0, jnp.minimum(j, nblk - 1), 0))
    t_spec = pl.BlockSpec((FFT_RADIX, FFT_BLK, LANES), lambda i, j: (0, jnp.minimum(j, nblk - 1), 0))
    return pl.pallas_call(
        _fft_kernel,
        grid=(nb, 2 * nblk),
        in_specs=[g_spec, g_spec, _const_spec((2 * rows, 2 * rows)), t_spec, t_spec,
                  _const_spec((rows, 2 * rows))],
        out_specs=pl.BlockSpec((1, FFT_RADIX, FFT_BLK, FNET_W),
                               lambda i, j: (i, 0, jnp.maximum(j - nblk, 0), 0)),
        out_shape=jax.ShapeDtypeStruct(gr.shape, F32),
        scratch_shapes=[pltpu.VMEM((FFT_RADIX, FFT_RADIX, FNET_W), F32)] * 2,
        compiler_params=_cparams(2),
        name="fft",
    )(gr, gi, k1, tc, ts, k2)


def _attend_head(qh, parts):
    scores = [_dot_nt(qh, kk) for kk, _ in parts]
    mx = functools.reduce(jnp.maximum, [jnp.max(s, axis=-1, keepdims=True) for s in scores])
    den = 0.0
    out = 0.0
    for s, (_, vv) in zip(scores, parts):
        p = jnp.exp(s - mx)
        den = den + jnp.sum(p, axis=-1, keepdims=True)
        out = out + _dot(p.astype(BF16), vv)
    return out / den


def _attend(q, kv_parts):
    heads = []
    for hd in range(N_Q_HEADS):
        g = hd // Q_PER_KV
        ksl = slice(g * HEAD_DIM, (g + 1) * HEAD_DIM)
        parts = [(kk[:, ksl], vv[:, ksl]) for kk, vv in kv_parts]
        heads.append(_attend_head(q[:, hd * HEAD_DIM:(hd + 1) * HEAD_DIM], parts))
    return jnp.concatenate(heads, axis=-1)


def _conv_module(win, rows, cw_ref, cb_ref, lg_ref, lb_ref):
    win_rows = rows + 2 * CONV_PAD
    shifted = [win] + [pltpu.roll(win, win_rows - sft, 0) for sft in range(1, SUBLANES)]
    cw = cw_ref[...]
    conv = jnp.zeros((rows, CONV_W), F32) + cb_ref[...]
    for j in range(CONV_K):
        off = j + CONV_PAD - CONV_K // 2
        base = (off // SUBLANES) * SUBLANES
        conv = conv + cw[j:j + 1] * shifted[off % SUBLANES][base:base + rows]
    mu = jnp.mean(conv, axis=-1, keepdims=True)
    cen = conv - mu
    var = jnp.mean(cen * cen, axis=-1, keepdims=True)
    conv = cen * lax.rsqrt(var + EPS) * lg_ref[...] + lb_ref[...]
    return conv * _sigmoid(conv)


def _spatial_gate(u, vn_ref, ws_ref, bs_ref, rows):
    lane = lax.broadcasted_iota(jnp.int32, (CHUNK, SGU_W), 1)
    mixed_chunks = []
    for c in range(rows // CHUNK):
        vn_c = vn_ref[c * CHUNK:(c + 1) * CHUNK, :]
        mixed = jnp.zeros((CHUNK, SGU_W), F32)
        for hd in range(SGU_HEADS):
            full = _dot(ws_ref[hd], vn_c)
            in_head = (lane >= hd * SGU_HEAD_W) & (lane < (hd + 1) * SGU_HEAD_W)
            mixed = jnp.where(in_head, full, mixed)
        mixed_chunks.append(mixed + bs_ref[...])
    return u * jnp.concatenate(mixed_chunks, axis=0)


def _mix_out(x, attn, conv, sgu, fnet, wo_ref, mod_ref, gn_ref):
    mix = (_dot(attn.astype(BF16), wo_ref[0:ATTN_W, :])
           + _dot(conv.astype(BF16), wo_ref[ATTN_W:ATTN_W + CONV_W, :])
           + _dot(sgu.astype(BF16), wo_ref[ATTN_W + CONV_W:ATTN_W + CONV_W + SGU_W, :])
           + _dot(fnet.astype(BF16), wo_ref[ATTN_W + CONV_W + SGU_W:, :]))
    gt1 = mod_ref[0][2:3]
    return x + gt1 * _rms(mix, gn_ref[1:2])


def _back_ctx_kernel(x_ref, mod_ref, gn_ref, q_ref, k_ref, v_ref, ci_ref, cw_ref, cb_ref,
                     lg_ref, lb_ref, u_ref, vn_ref, ws_ref, bs_ref, gr_ref, gi_ref,
                     c_ref, s_ref, wo_ref, o_ref, *, seq):
    rows = x_ref.shape[0]
    halo = jnp.zeros((CONV_PAD, CONV_W), F32)
    attn, conv, fnet = [], [], []
    for i in range(rows // seq):
        rs = slice(i * seq, (i + 1) * seq)
        kv = (k_ref[i, 0].astype(BF16), v_ref[i, 0].astype(BF16))
        attn.append(_attend(q_ref[rs, :], [kv]))
        win = jnp.concatenate([halo, ci_ref[rs, :], halo], axis=0)
        conv.append(_conv_module(win, seq, cw_ref, cb_ref, lg_ref, lb_ref))
        fnet.append((_dot(c_ref[...], gr_ref[rs, :]) + _dot(s_ref[...], gi_ref[rs, :]))
                    * (1.0 / np.sqrt(seq * FNET_GROUP_W)))
    cat = lambda parts: jnp.concatenate(parts, axis=0)
    sgu = _spatial_gate(u_ref[...], vn_ref, ws_ref, bs_ref, rows)
    o_ref[...] = _mix_out(x_ref[...], cat(attn), cat(conv), sgu, cat(fnet), wo_ref, mod_ref, gn_ref)


def _back_lat_kernel(x_ref, mod_ref, gn_ref, q_ref, k_ref, v_ref, ck_ref, cv_ref,
                     ci_ref, cip_ref, cin_ref, cw_ref, cb_ref, lg_ref, lb_ref,
                     u_ref, vn_ref, ws_ref, bs_ref, fn_ref, wo_ref, o_ref, *, per_seq):
    rows = x_ref.shape[0]
    t = pl.program_id(0)
    attn = _attend(q_ref[...], [(k_ref[...], v_ref[...]), (ck_ref[0], cv_ref[0])])
    has_prev = (lax.rem(t, per_seq) > 0).astype(F32)
    has_next = (lax.rem(t, per_seq) < per_seq - 1).astype(F32)
    win = jnp.concatenate([cip_ref[...] * has_prev, ci_ref[...], cin_ref[...] * has_next], axis=0)
    conv = _conv_module(win, rows, cw_ref, cb_ref, lg_ref, lb_ref)
    sgu = _spatial_gate(u_ref[...], vn_ref, ws_ref, bs_ref, rows)
    o_ref[...] = _mix_out(x_ref[...], attn, conv, sgu, fn_ref[...], wo_ref, mod_ref, gn_ref)


def _back_ctx(x, mod, gn, q, kc, vc, ci, cw, cb, lg, lb, u, vn, ws, bs, gr, gi, cm, sm, wo,
              *, seq, layer):
    n = x.shape[0]
    tq = ROW_TILE
    per_tile = tq // seq
    row = lambda w: pl.BlockSpec((tq, w), lambda t: (t, 0))
    cache_spec = pl.BlockSpec((per_tile, 1, seq, KV_W), lambda t: (t, layer, 0, 0))
    in_specs = [row(D_MODEL), pl.BlockSpec((1, 6, D_MODEL), lambda t: (0, 0, 0)),
                _const_spec((4, D_MODEL)), row(ATTN_W), cache_spec, cache_spec, row(CONV_W),
                _const_spec(cw.shape), _const_spec((1, CONV_W)), _const_spec((1, CONV_W)),
                _const_spec((1, CONV_W)), row(SGU_W), row(SGU_W),
                _const_spec((SGU_HEADS, CHUNK, CHUNK)), _const_spec((CHUNK, SGU_W)),
                row(FNET_W), row(FNET_W), _const_spec((seq, seq)), _const_spec((seq, seq)),
                _const_spec((D_MODEL, D_MODEL))]
    return pl.pallas_call(
        functools.partial(_back_ctx_kernel, seq=seq),
        grid=(n // tq,),
        in_specs=in_specs,
        out_specs=row(D_MODEL),
        out_shape=jax.ShapeDtypeStruct(x.shape, F32),
        compiler_params=_cparams(1),
        name="back_ctx",
    )(x, mod, gn, q, kc, vc, ci, cw, cb, lg, lb, u, vn, ws, bs, gr, gi, cm, sm, wo)


def _back_lat(x, mod, gn, q, k, v, ck, cv, ci, cw, cb, lg, lb, u, vn, ws, bs, fn, wo, *, seq):
    n = x.shape[0]
    tq = LAT_TQ
    nt = n // tq
    per_seq = seq // tq
    nmod = mod.shape[0]
    past = ck.shape[1]
    halo_per = tq // CONV_PAD
    last_halo = n // CONV_PAD - 1
    row = lambda w: pl.BlockSpec((tq, w), lambda t: (t, 0))
    in_specs = [row(D_MODEL), pl.BlockSpec((1, 6, D_MODEL), lambda t: ((t * nmod) // nt, 0, 0)),
                _const_spec((4, D_MODEL)), row(ATTN_W),
                pl.BlockSpec((seq, KV_W), lambda t: (t // per_seq, 0)),
                pl.BlockSpec((seq, KV_W), lambda t: (t // per_seq, 0)),
                pl.BlockSpec((1, past, KV_W), lambda t: (t // per_seq, 0, 0)),
                pl.BlockSpec((1, past, KV_W), lambda t: (t // per_seq, 0, 0)),
                row(CONV_W),
                pl.BlockSpec((CONV_PAD, CONV_W), lambda t: (jnp.maximum(t * halo_per - 1, 0), 0)),
                pl.BlockSpec((CONV_PAD, CONV_W),
                             lambda t: (jnp.minimum((t + 1) * halo_per, last_halo), 0)),
                _const_spec(cw.shape), _const_spec((1, CONV_W)), _const_spec((1, CONV_W)),
                _const_spec((1, CONV_W)), row(SGU_W), row(SGU_W),
                _const_spec((SGU_HEADS, CHUNK, CHUNK)), _const_spec((CHUNK, SGU_W)),
                row(FNET_W), _const_spec((D_MODEL, D_MODEL))]
    return pl.pallas_call(
        functools.partial(_back_lat_kernel, per_seq=per_seq),
        grid=(nt,),
        in_specs=in_specs,
        out_specs=row(D_MODEL),
        out_shape=jax.ShapeDtypeStruct(x.shape, F32),
        compiler_params=_cparams(1),
        name="back_lat",
    )(x, mod, gn, q, k, v, ck, cv, ci, ci, ci, cw, cb, lg, lb, u, vn, ws, bs, fn, wo)


def _ffn_kernel(x_ref, xp_ref, xn_ref, mod_ref, gn_ref, wup_ref, cw_ref, cb_ref, wdn_ref, o_ref,
                xs_ref, act_ref, os_ref, *, seq):
    t = pl.program_id(0)
    tm = x_ref.shape[0]
    ext = tm + 2 * SUBLANES
    seg = ext // SUBLANES
    nslab = D_MODEL // LANES
    m = mod_ref[0]
    sh2, sc2, gt2 = m[3:4], m[4:5], m[5:6]
    for c in range(nslab):
        ls = slice(c * LANES, (c + 1) * LANES)
        xs_ref[c, 0:SUBLANES, :] = xp_ref[:, ls]
        xs_ref[c, SUBLANES:SUBLANES + tm, :] = x_ref[:, ls]
        xs_ref[c, SUBLANES + tm:ext, :] = xn_ref[:, ls]
    xq = jnp.concatenate(
        [jnp.concatenate([xs_ref[c, pl.ds(i, SUBLANES, stride=seg), :] for i in range(seg)], axis=0)
         for c in range(nslab)], axis=1)
    he = _rms(xq, gn_ref[2:3]) * (1.0 + sc2) + sh2
    p = lax.broadcasted_iota(jnp.int32, (ext, D_MODEL), 0)
    tile_row = (p & (SUBLANES - 1)) * seg + (p >> 3)
    lo = jnp.where(lax.rem(t * tm, seq) == 0, SUBLANES, 0)
    hi = jnp.where(lax.rem((t + 1) * tm, seq) == 0, tm + SUBLANES, ext)
    hb = jnp.where((tile_row >= lo) & (tile_row < hi), he, 0.0).astype(BF16)
    starts = [SUBLANES + r for r in range(seq, tm, seq)]
    assert all(r % seg == 0 for r in starts)
    sub = lax.broadcasted_iota(jnp.int32, (SUBLANES, FFN_CHUNK), 0)

    def conv3(up, col0):
        w = cw_ref[:, col0:col0 + FFN_CHUNK]
        wrap_prev = pltpu.roll(up[ext - SUBLANES:ext], 1, 0)
        wrap_next = pltpu.roll(up[0:SUBLANES], SUBLANES - 1, 0)
        for r in starts:
            wrap_prev = jnp.where(sub == r // seg, 0.0, wrap_prev)
            wrap_next = jnp.where(sub == r // seg - 1, 0.0, wrap_next)
        prev = jnp.concatenate([wrap_prev, up[0:ext - SUBLANES]], axis=0)
        nxt = jnp.concatenate([up[SUBLANES:ext], wrap_next], axis=0)
        return w[0:1] * prev + w[1:2] * up + w[2:3] * nxt + cb_ref[:, col0:col0 + FFN_CHUNK]

    for c in range(D_FF // FFN_CHUNK):
        ca = c * FFN_CHUNK
        cg = D_FF + c * FFN_CHUNK
        a = conv3(_dot(hb, wup_ref[:, ca:ca + FFN_CHUNK]), ca)
        g = conv3(_dot(hb, wup_ref[:, cg:cg + FFN_CHUNK]), cg)
        act_ref[:, ca:ca + FFN_CHUNK] = (a * _sigmoid(a) * g).astype(BF16)
    y = xq + gt2 * _rms(_dot(act_ref[...], wdn_ref[...]), gn_ref[3:4])
    for c in range(nslab):
        for i in range(seg):
            os_ref[c, pl.ds(i, SUBLANES, stride=seg), :] = (
                y[i * SUBLANES:(i + 1) * SUBLANES, c * LANES:(c + 1) * LANES])
    for c in range(nslab):
        o_ref[:, c * LANES:(c + 1) * LANES] = os_ref[c, SUBLANES:SUBLANES + tm, :]


def _ffn(x, mod, gn, w_up, cw, cb, w_dn, *, seq):
    n = x.shape[0]
    tm = ROW_TILE
    nt = n // tm
    nmod = mod.shape[0]
    per8 = tm // SUBLANES
    last8 = n // SUBLANES - 1
    ext = tm + 2 * SUBLANES
    slab = pltpu.VMEM((D_MODEL // LANES, ext, LANES), F32)
    return pl.pallas_call(
        functools.partial(_ffn_kernel, seq=seq),
        grid=(nt,),
        in_specs=[pl.BlockSpec((tm, D_MODEL), lambda t: (t, 0)),
                  pl.BlockSpec((SUBLANES, D_MODEL), lambda t: (jnp.maximum(t * per8 - 1, 0), 0)),
                  pl.BlockSpec((SUBLANES, D_MODEL),
                               lambda t: (jnp.minimum((t + 1) * per8, last8), 0)),
                  pl.BlockSpec((1, 6, D_MODEL), lambda t: ((t * nmod) // nt, 0, 0)),
                  _const_spec((4, D_MODEL)),
                  _const_spec((D_MODEL, 2 * D_FF)),
                  _const_spec((SUBLANES, 2 * D_FF)), _const_spec((1, 2 * D_FF)),
                  _const_spec((D_FF, D_MODEL))],
        out_specs=pl.BlockSpec((tm, D_MODEL), lambda t: (t, 0)),
        out_shape=jax.ShapeDtypeStruct(x.shape, F32),
        scratch_shapes=[slab, pltpu.VMEM((ext, D_FF), BF16), slab],
        compiler_params=_cparams(1),
        name="ffn",
    )(x, x, x, mod, gn, w_up, cw, cb, w_dn)


def _np_consts(ctx_seq):
    hd = np.arange(ATTN_W) // HEAD_DIM
    pmat = (hd[:, None] == hd[None, :]).astype(np.float32) / HEAD_DIM
    ch = np.arange(FNET_W)
    same = (ch[:, None] // FNET_GROUP_W) == (ch[None, :] // FNET_GROUP_W)
    ang = 2.0 * np.pi * ((ch[:, None] % FNET_GROUP_W) * (ch[None, :] % FNET_GROUP_W)) / FNET_GROUP_W
    cc = np.where(same, np.cos(ang), 0.0).astype(np.float32)
    cs = np.where(same, np.sin(ang), 0.0).astype(np.float32)
    n = np.arange(ctx_seq)
    ang = 2.0 * np.pi * ((n[:, None] * n[None, :]) % ctx_seq) / ctx_seq
    c_ctx = np.cos(ang).astype(np.float32)
    s_ctx = np.sin(ang).astype(np.float32)
    r = np.arange(FFT_RADIX)
    ang = 2.0 * np.pi * ((r[:, None] * r[None, :]) % FFT_RADIX) / FFT_RADIX
    c64, s64 = np.cos(ang), np.sin(ang)
    eye = np.eye(FFT_BLK)
    kc, ks = np.kron(c64, eye), np.kron(s64, eye)
    k1 = np.block([[kc, ks], [-ks, kc]]).astype(np.float32)
    n_seq = FFT_RADIX * FFT_RADIX
    ang = 2.0 * np.pi * (r[:, None] * r[None, :]) / n_seq
    tc = np.repeat(np.cos(ang)[:, :, None], LANES, axis=2).astype(np.float32)
    ts = np.repeat(np.sin(ang)[:, :, None], LANES, axis=2).astype(np.float32)
    scale = 1.0 / np.sqrt(n_seq * FNET_GROUP_W)
    k2c = np.einsum("db,ec->decb", c64, eye).reshape(FFT_RADIX * FFT_BLK, FFT_BLK * FFT_RADIX)
    k2s = np.einsum("db,ec->decb", s64, eye).reshape(FFT_RADIX * FFT_BLK, FFT_BLK * FFT_RADIX)
    k2 = (np.concatenate([k2c, k2s], axis=1) * scale).astype(np.float32)
    return pmat, cc, cs, c_ctx, s_ctx, k1, k2, tc, ts


def _rope_tables(s):
    rows = s // GRID_W
    row = jnp.repeat(jnp.arange(rows, dtype=F32), GRID_W)
    col = jnp.tile(jnp.arange(GRID_W, dtype=F32), rows)
    n_f = HEAD_DIM // 4
    inv = ROPE_THETA ** (-jnp.arange(n_f, dtype=F32) / n_f)
    ang = jnp.concatenate([row[:, None] * inv, col[:, None] * inv], axis=-1)
    cos = jnp.repeat(jnp.cos(ang), 2, axis=-1)
    sin = jnp.repeat(jnp.sin(ang), 2, axis=-1)
    sign = jnp.tile(jnp.array([-1.0, 1.0], F32), HEAD_DIM // 2)
    return jnp.tile(cos, (1, N_Q_HEADS)), jnp.tile(sin * sign, (1, N_Q_HEADS))


def kernel(x_prompt, x_sample, cache_k, cache_v, c, c_ctx, w_ada, b_ada, g_norm, w_in, g_q, g_k, conv_w, conv_b, conv_ln_g, conv_ln_b, sgu_g, w_s, b_s, w_out, w_up, ffn_conv_w, ffn_conv_b, w_down):
    bsz, seq, _ = x_prompt.shape
    dec_b, dec_s, _ = x_sample.shape
    past = cache_k.shape[2]
    assert dec_s == FFT_RADIX * FFT_RADIX and ROW_TILE % seq == 0 and dec_s % ROW_TILE == 0

    consts = _np_consts(seq)
    pmat, cc, cs, c_ctx_m, s_ctx_m, k1, k2 = (jnp.asarray(a).astype(BF16) for a in consts[:7])
    tc, ts = jnp.asarray(consts[7]), jnp.asarray(consts[8])
    rope = _rope_tables(dec_s)

    cond8 = jnp.zeros((SUBLANES, D_MODEL), F32).at[0].set(c_ctx).at[1:1 + dec_b].set(c)
    mod = _ada(cond8, w_ada, b_ada).reshape(DEPTH, SUBLANES, 6, D_MODEL)

    w_in_b, w_out_b, w_up_b, w_dn_b = (w.astype(BF16) for w in (w_in, w_out, w_up, w_down))
    ws_b = w_s.astype(BF16)
    cw_pad = jnp.pad(conv_w, ((0, 0), (0, 32 - CONV_K), (0, 0)))
    fcw_pad = jnp.pad(ffn_conv_w, ((0, 0), (0, SUBLANES - FFN_CONV_K), (0, 0)))
    bs_full = jnp.repeat(jnp.swapaxes(b_s, 1, 2), SGU_HEAD_W, axis=2)
    ck = cache_k.reshape(dec_b, DEPTH, past, KV_W).astype(BF16)
    cv = cache_v.reshape(dec_b, DEPTH, past, KV_W).astype(BF16)
    row1 = lambda a, l: a[l][None, :]

    xp = x_prompt.reshape(bsz * seq, D_MODEL)
    xs = x_sample.reshape(dec_b * dec_s, D_MODEL)
    kv_ctx = None
    for l in range(DEPTH):
        gq = jnp.tile(g_q[l], N_Q_HEADS)[None, :]
        gk = jnp.tile(g_k[l], N_KV_HEADS)[None, :]
        common = (g_norm[l], w_in_b[l], gq, gk, pmat, row1(sgu_g, l), cc, cs)
        conv_args = (cw_pad[l], row1(conv_b, l), row1(conv_ln_g, l), row1(conv_ln_b, l))
        ffn_args = (g_norm[l], w_up_b[l], fcw_pad[l], row1(ffn_conv_b, l), w_dn_b[l])

        mod_c = mod[l, 0:1]
        q, kc, vc, ci, u, vn, gr, gi = _front(xp, mod_c, *common, None, kv_ctx,
                                              latent=False, seq=seq, layer=l)
        kv_ctx = (kc, vc)
        xp = _back_ctx(xp, mod_c, g_norm[l], q, kc, vc, ci, *conv_args, u, vn, ws_b[l], bs_full[l],
                       gr, gi, c_ctx_m, s_ctx_m, w_out_b[l], seq=seq, layer=l)
        xp = _ffn(xp, mod_c, *ffn_args, seq=seq)

        mod_s = mod[l, 1:1 + dec_b]
        q, k, v, ci, u, vn, gr, gi = _front(xs, mod_s, *common, rope, None,
                                             latent=True, seq=dec_s, layer=l)
        grid4 = (dec_b, FFT_RADIX, FFT_RADIX, FNET_W)
        fn = _fft(gr.reshape(grid4), gi.reshape(grid4), k1, tc, ts, k2).reshape(-1, FNET_W)
        xs = _back_lat(xs, mod_s, g_norm[l], q, k, v, ck[:, l], cv[:, l], ci, *conv_args, u, vn,
                       ws_b[l], bs_full[l], fn, w_out_b[l], seq=dec_s)
        xs = _ffn(xs, mod_s, *ffn_args, seq=dec_s)

    new_k = kv_ctx[0].reshape(bsz, DEPTH, seq, N_KV_HEADS, HEAD_DIM)
    new_v = kv_ctx[1].reshape(bsz, DEPTH, seq, N_KV_HEADS, HEAD_DIM)
    return (xp.reshape(x_prompt.shape), xs.reshape(x_sample.shape), new_k, new_v)
```

```python
import functools

import numpy as np
import jax
import jax.numpy as jnp
from jax import lax
from jax.experimental import pallas as pl
from jax.experimental.pallas import tpu as pltpu

D_MODEL = 1024
DEPTH = 2
GRID_W = 64
HEAD_DIM = 64
N_Q_HEADS = 4
N_KV_HEADS = 2
Q_PER_KV = N_Q_HEADS // N_KV_HEADS
ATTN_W = N_Q_HEADS * HEAD_DIM
KV_W = N_KV_HEADS * HEAD_DIM
ATTN_SCALE = HEAD_DIM ** -0.5
Q_SCALE = ATTN_SCALE * float(np.log2(np.e))
VT_ROWS = 80
KEY_CHUNK = 512
ATTN_LOOKAHEAD = 3
ROPE_THETA = 10000.0
CONV_W = 256
CONV_K = 31
CONV_PAD = 16
SGU_W = 256
SGU_HEADS = 4
SGU_HEAD_W = SGU_W // SGU_HEADS
CHUNK = 128
FNET_W = 256
FNET_GROUPS = 4
FNET_GROUP_W = FNET_W // FNET_GROUPS
D_FF = 2816
FFN_CONV_K = 3
EPS = 1e-6
SPLIT_Q = ATTN_W
SPLIT_K = SPLIT_Q + KV_W
SPLIT_V = SPLIT_K + KV_W
SPLIT_CONV = SPLIT_V + 2 * CONV_W
SPLIT_SGU = SPLIT_CONV + 2 * SGU_W
IN_COLS = SPLIT_SGU + FNET_W

SUBLANES = 8
LANES = 128
ROW_TILE = 512
LAT_TQ = 256
FFN_CHUNK = 256
FFT_RADIX = 64
FFT_BLK = 8
VMEM_LIMIT = 56 * 1024 * 1024

F32 = jnp.float32
BF16 = jnp.bfloat16


def _cparams(n_axes):
    return pltpu.CompilerParams(dimension_semantics=("arbitrary",) * n_axes,
                                vmem_limit_bytes=VMEM_LIMIT)


def _const_spec(shape):
    nd = len(shape)
    return pl.BlockSpec(shape, lambda *_: (0,) * nd, pipeline_mode=pl.Buffered(1))


def _dot(a, b):
    return jnp.dot(a, b, preferred_element_type=F32)


def _dot_nt(a, b):
    return lax.dot_general(a, b, (((1,), (1,)), ((), ())), preferred_element_type=F32)


def _sigmoid(x):
    return 1.0 / (1.0 + jnp.exp(-x))


def _rms(x, g):
    ms = jnp.mean(x * x, axis=-1, keepdims=True)
    return x * lax.rsqrt(ms + EPS) * g


def _head_rms(z, pmat, g):
    sq = z * z
    hi = sq.astype(BF16)
    lo = (sq - hi.astype(F32)).astype(BF16)
    ms = _dot(hi, pmat) + _dot(lo, pmat)
    return z * lax.rsqrt(ms + EPS) * g


def _rope(x, cos, sin_signed):
    w = x.shape[1]
    nxt = pltpu.roll(x, w - 1, 1)
    prv = pltpu.roll(x, 1, 1)
    lane = lax.broadcasted_iota(jnp.int32, x.shape, 1)
    swapped = jnp.where((lane & 1) == 0, nxt, prv)
    return x * cos + swapped * sin_signed


def _ada_kernel(cond_ref, w_ref, b_ref, o_ref):
    c = cond_ref[...]
    s = (c * _sigmoid(c)).astype(BF16)
    o_ref[0] = _dot(s, w_ref[0].astype(BF16)) + b_ref[0]


def _ada(cond8, w_ada, b_ada):
    tn = 1536
    n_out = 6 * D_MODEL
    return pl.pallas_call(
        _ada_kernel,
        grid=(DEPTH, n_out // tn),
        in_specs=[pl.BlockSpec((SUBLANES, D_MODEL), lambda l, j: (0, 0)),
                  pl.BlockSpec((1, D_MODEL, tn), lambda l, j: (l, 0, j)),
                  pl.BlockSpec((1, 1, tn), lambda l, j: (l, 0, j))],
        out_specs=pl.BlockSpec((1, SUBLANES, tn), lambda l, j: (l, 0, j)),
        out_shape=jax.ShapeDtypeStruct((DEPTH, SUBLANES, n_out), F32),
        compiler_params=_cparams(2),
        name="ada",
    )(cond8, w_ada, b_ada.reshape(DEPTH, 1, n_out))


def _front_kernel(*refs, latent, seq):
    (x_ref, mod_ref, gn_ref, win_ref, gq_ref, gk_ref, pm_ref, sg_ref, cc_ref, cs_ref) = refs[:10]
    if latent:
        cos_ref, sin_ref = refs[10:12]
        q_ref, k_ref, v_ref, ci_ref, u_ref, vn_ref, gr_ref, gi_ref = refs[12:]
    else:
        q_ref, k_ref, v_ref, ci_ref, u_ref, vn_ref, gr_ref, gi_ref = refs[10:]
    m = mod_ref[0]
    sh1, sc1 = m[0:1], m[1:2]
    h = _rms(x_ref[...], gn_ref[0:1]) * (1.0 + sc1) + sh1
    z = _dot(h.astype(BF16), win_ref[...])
    pm = pm_ref[...]
    q = _head_rms(z[:, :SPLIT_Q], pm, gq_ref[...])
    k = _head_rms(z[:, SPLIT_Q:SPLIT_K], pm[:KV_W, :KV_W], gk_ref[...])
    v = z[:, SPLIT_K:SPLIT_V]
    if latent:
        cos = cos_ref[...]
        sin = sin_ref[...]
        q = _rope(q, cos, sin)
        k = _rope(k, cos[:, :KV_W], sin[:, :KV_W])
        k_ref[...] = k.astype(k_ref.dtype)
        for g, ext in enumerate(_vt_ext(v)):
            v_ref[0, g] = ext
    else:
        rows = x_ref.shape[0]
        k_ref[:, 0] = k.reshape(rows // seq, seq, KV_W)
        v_ref[:, 0] = v.reshape(rows // seq, seq, KV_W)
    q_ref[...] = (q * Q_SCALE).astype(q_ref.dtype)
    a = z[:, SPLIT_V:SPLIT_V + CONV_W]
    gt = z[:, SPLIT_V + CONV_W:SPLIT_CONV]
    ci_ref[...] = a * _sigmoid(gt)
    u_ref[...] = z[:, SPLIT_CONV:SPLIT_CONV + SGU_W]
    vn_ref[...] = _rms(z[:, SPLIT_CONV + SGU_W:SPLIT_SGU], sg_ref[...]).astype(vn_ref.dtype)
    ff = z[:, SPLIT_SGU:].astype(BF16)
    gr_ref[...] = _dot(ff, cc_ref[...]).astype(gr_ref.dtype)
    gi_ref[...] = (-_dot(ff, cs_ref[...])).astype(gi_ref.dtype)


def _front(x, mod, gn, w_in, gq, gk, pm, sgu_g, cc, cs, rope, kv_prev, *, latent, seq, layer):
    n = x.shape[0]
    tm = ROW_TILE
    nt = n // tm
    nmod = mod.shape[0]
    nb = n // seq
    row = lambda w: pl.BlockSpec((tm, w), lambda t: (t, 0))
    in_specs = [row(D_MODEL),
                pl.BlockSpec((1, 6, D_MODEL), lambda t: ((t * nmod) // nt, 0, 0)),
                _const_spec((4, D_MODEL)),
                _const_spec((D_MODEL, IN_COLS)),
                _const_spec((1, ATTN_W)), _const_spec((1, KV_W)),
                _const_spec((ATTN_W, ATTN_W)), _const_spec((1, SGU_W)),
                _const_spec((FNET_W, FNET_W)), _const_spec((FNET_W, FNET_W))]
    args = [x, mod, gn, w_in, gq, gk, pm, sgu_g, cc, cs]
    aliases = {}
    if latent:
        per_seq = seq // tm
        in_specs += [pl.BlockSpec((tm, ATTN_W), lambda t: (t % per_seq, 0))] * 2
        args += list(rope)
        kv_specs = [row(KV_W), pl.BlockSpec((1, N_KV_HEADS, VT_ROWS, tm),
                                            lambda t: (t // per_seq, 0, 0, t % per_seq))]
        kv_shapes = [jax.ShapeDtypeStruct((n, KV_W), BF16),
                     jax.ShapeDtypeStruct((nb, N_KV_HEADS, VT_ROWS, seq), BF16)]
        f_dt = F32
    else:
        per_tile = tm // seq
        cache_spec = pl.BlockSpec((per_tile, 1, seq, KV_W), lambda t: (t, layer, 0, 0))
        kv_specs = [cache_spec, cache_spec]
        kv_shapes = [jax.ShapeDtypeStruct((nb, DEPTH, seq, KV_W), F32)] * 2
        if kv_prev is not None:
            in_specs += [pl.BlockSpec(memory_space=pl.ANY)] * 2
            args += list(kv_prev)
            aliases = {len(args) - 2: 1, len(args) - 1: 2}
        f_dt = BF16
    outs = [(ATTN_W, BF16), None, None, (CONV_W, F32), (SGU_W, F32), (SGU_W, BF16),
            (FNET_W, f_dt), (FNET_W, f_dt)]
    out_specs = [row(o[0]) if o else None for o in outs]
    out_shape = [jax.ShapeDtypeStruct((n, o[0]), o[1]) if o else None for o in outs]
    out_specs[1:3] = kv_specs
    out_shape[1:3] = kv_shapes

    def body(*refs):
        if not latent and kv_prev is not None:
            refs = refs[:10] + refs[12:]
        _front_kernel(*refs, latent=latent, seq=seq)

    return pl.pallas_call(
        body,
        grid=(nt,),
        in_specs=in_specs,
        out_specs=out_specs,
        out_shape=out_shape,
        input_output_aliases=aliases,
        compiler_params=_cparams(1),
        name="front_lat" if latent else "front_ctx",
    )(*args)


def _fft_kernel(gr_ref, gi_ref, k1_ref, tc_ref, ts_ref, k2_ref, o_ref, yr_ref, yi_ref):
    j = pl.program_id(1)
    nblk = FFT_RADIX // FFT_BLK
    rows = FFT_RADIX * FFT_BLK

    @pl.when(j < nblk)
    def _():
        g = jnp.concatenate([gr_ref[0].reshape(rows, FNET_W), gi_ref[0].reshape(rows, FNET_W)],
                            axis=0).astype(BF16)
        y = _dot(k1_ref[...], g)
        yr, yi = y[:rows], y[rows:]
        tc = tc_ref[...].reshape(rows, LANES)
        ts = ts_ref[...].reshape(rows, LANES)
        tc = jnp.concatenate([tc] * (FNET_W // LANES), axis=1)
        ts = jnp.concatenate([ts] * (FNET_W // LANES), axis=1)
        col = pl.ds(pl.multiple_of(j * FFT_BLK, FFT_BLK), FFT_BLK)
        yr_ref[:, col, :] = (yr * tc + yi * ts).reshape(FFT_RADIX, FFT_BLK, FNET_W)
        yi_ref[:, col, :] = (yi * tc - yr * ts).reshape(FFT_RADIX, FFT_BLK, FNET_W)

    @pl.when(j >= nblk)
    def _():
        blk = pl.ds(pl.multiple_of((j - nblk) * FFT_BLK, FFT_BLK), FFT_BLK)
        y = jnp.concatenate([yr_ref[blk].reshape(rows, FNET_W), yi_ref[blk].reshape(rows, FNET_W)],
                            axis=0).astype(BF16)
        o_ref[0] = _dot(k2_ref[...], y).reshape(FFT_RADIX, FFT_BLK, FNET_W)


def _fft(gr, gi, k1, tc, ts, k2):
    nb = gr.shape[0]
    nblk = FFT_RADIX // FFT_BLK
    rows = FFT_RADIX * FFT_BLK
    g_spec = pl.BlockSpec((1, FFT_RADIX, FFT_BLK, FNET_W),
                          lambda i, j: (i, 0, jnp.minimum(j, nblk - 1), 0))
    t_spec = pl.BlockSpec((FFT_RADIX, FFT_BLK, LANES), lambda i, j: (0, jnp.minimum(j, nblk - 1), 0))
    return pl.pallas_call(
        _fft_kernel,
        grid=(nb, 2 * nblk),
        in_specs=[g_spec, g_spec, _const_spec((2 * rows, 2 * rows)), t_spec, t_spec,
                  _const_spec((rows, 2 * rows))],
        out_specs=pl.BlockSpec((1, FFT_RADIX, FFT_BLK, FNET_W),
                               lambda i, j: (i, 0, jnp.maximum(j - nblk, 0), 0)),
        out_shape=jax.ShapeDtypeStruct(gr.shape, F32),
        scratch_shapes=[pltpu.VMEM((FFT_RADIX, FFT_RADIX, FNET_W), F32)] * 2,
        compiler_params=_cparams(2),
        name="fft",
    )(gr, gi, k1, tc, ts, k2)


def _vt_ext(v):
    vt = v.T
    n = v.shape[0]
    pad = jnp.where(lax.broadcasted_iota(jnp.int32, (VT_ROWS - HEAD_DIM, n), 0) == 0, 1.0, 0.0)
    return [jnp.concatenate([vt[g * HEAD_DIM:(g + 1) * HEAD_DIM], pad], axis=0).astype(BF16)
            for g in range(N_KV_HEADS)]


def _attend(q, k_chunks, vt_chunks):
    lane = lax.broadcasted_iota(jnp.int32, (q.shape[0], KV_W), 1)
    qm = {}
    for half in range(Q_PER_KV):
        q2 = q[:, half * KV_W:(half + 1) * KV_W]
        for g in range(N_KV_HEADS):
            in_head = (lane >= g * HEAD_DIM) & (lane < (g + 1) * HEAD_DIM)
            qm[half, g] = jnp.where(in_head, q2, jnp.zeros_like(q2))
    items = [(c, hg) for c in range(len(k_chunks)) for hg in qm]
    state = {}

    def consume(c, hg, s):
        vt = vt_chunks[c][hg[1]]
        cm = jnp.max(s, axis=0, keepdims=True)
        if hg not in state:
            state[hg] = (cm, _dot(vt, jnp.exp2(s - cm).astype(BF16)))
        else:
            m, acc = state[hg]
            m_new = jnp.maximum(m, cm)
            state[hg] = (m_new, acc * jnp.exp2(m - m_new)
                         + _dot(vt, jnp.exp2(s - m_new).astype(BF16)))

    pending = []
    for c, hg in items:
        pending.append((c, hg, _dot_nt(k_chunks[c], qm[hg])))
        if len(pending) > ATTN_LOOKAHEAD:
            consume(*pending.pop(0))
    for item in pending:
        consume(*item)
    outs = {}
    for (half, g), (_, acc) in state.items():
        outs[Q_PER_KV * g + half] = acc[:HEAD_DIM] / acc[HEAD_DIM:HEAD_DIM + 1]
    return jnp.concatenate([outs[h] for h in range(N_Q_HEADS)], axis=0).T


def _conv_module(win, rows, cw_ref, cb_ref, lg_ref, lb_ref):
    win_rows = rows + 2 * CONV_PAD
    shifted = [win] + [pltpu.roll(win, win_rows - sft, 0) for sft in range(1, SUBLANES)]
    cw = cw_ref[...]
    conv = jnp.zeros((rows, CONV_W), F32) + cb_ref[...]
    for j in range(CONV_K):
        off = j + CONV_PAD - CONV_K // 2
        base = (off // SUBLANES) * SUBLANES
        conv = conv + cw[j:j + 1] * shifted[off % SUBLANES][base:base + rows]
    mu = jnp.mean(conv, axis=-1, keepdims=True)
    cen = conv - mu
    var = jnp.mean(cen * cen, axis=-1, keepdims=True)
    conv = cen * lax.rsqrt(var + EPS) * lg_ref[...] + lb_ref[...]
    return conv * _sigmoid(conv)


def _spatial_gate(u, vn_ref, ws_ref, bs_ref, rows):
    lane = lax.broadcasted_iota(jnp.int32, (CHUNK, SGU_W), 1)
    mixed_chunks = []
    for c in range(rows // CHUNK):
        vn_c = vn_ref[c * CHUNK:(c + 1) * CHUNK, :]
        mixed = jnp.zeros((CHUNK, SGU_W), F32)
        for hd in range(SGU_HEADS):
            full = _dot(ws_ref[hd], vn_c)
            in_head = (lane >= hd * SGU_HEAD_W) & (lane < (hd + 1) * SGU_HEAD_W)
            mixed = jnp.where(in_head, full, mixed)
        mixed_chunks.append(mixed + bs_ref[...])
    return u * jnp.concatenate(mixed_chunks, axis=0)


def _mix_out(x, attn, conv, sgu, fnet, wo_ref, mod_ref, gn_ref):
    mix = (_dot(attn.astype(BF16), wo_ref[0:ATTN_W, :])
           + _dot(conv.astype(BF16), wo_ref[ATTN_W:ATTN_W + CONV_W, :])
           + _dot(sgu.astype(BF16), wo_ref[ATTN_W + CONV_W:ATTN_W + CONV_W + SGU_W, :])
           + _dot(fnet.astype(BF16), wo_ref[ATTN_W + CONV_W + SGU_W:, :]))
    gt1 = mod_ref[0][2:3]
    return x + gt1 * _rms(mix, gn_ref[1:2])


def _back_ctx_kernel(x_ref, mod_ref, gn_ref, q_ref, k_ref, v_ref, ci_ref, cw_ref, cb_ref,
                     lg_ref, lb_ref, u_ref, vn_ref, ws_ref, bs_ref, gr_ref, gi_ref,
                     c_ref, s_ref, wo_ref, o_ref, *, seq):
    rows = x_ref.shape[0]
    halo = jnp.zeros((CONV_PAD, CONV_W), F32)
    attn, conv, fnet = [], [], []
    for i in range(rows // seq):
        rs = slice(i * seq, (i + 1) * seq)
        attn.append(_attend(q_ref[rs, :], [k_ref[i, 0].astype(BF16)], [_vt_ext(v_ref[i, 0])]))
        win = jnp.concatenate([halo, ci_ref[rs, :], halo], axis=0)
        conv.append(_conv_module(win, seq, cw_ref, cb_ref, lg_ref, lb_ref))
        fnet.append((_dot(c_ref[...], gr_ref[rs, :]) + _dot(s_ref[...], gi_ref[rs, :]))
                    * (1.0 / np.sqrt(seq * FNET_GROUP_W)))
    cat = lambda parts: jnp.concatenate(parts, axis=0)
    sgu = _spatial_gate(u_ref[...], vn_ref, ws_ref, bs_ref, rows)
    o_ref[...] = _mix_out(x_ref[...], cat(attn), cat(conv), sgu, cat(fnet), wo_ref, mod_ref, gn_ref)


def _back_lat_kernel(x_ref, mod_ref, gn_ref, q_ref, k_ref, v_ref, ck_ref, cv_ref,
                     ci_ref, cip_ref, cin_ref, cw_ref, cb_ref, lg_ref, lb_ref,
                     u_ref, vn_ref, ws_ref, bs_ref, fn_ref, wo_ref, o_ref, *, per_seq):
    rows = x_ref.shape[0]
    t = pl.program_id(0)
    chunks = [slice(c, c + KEY_CHUNK) for c in range(0, k_ref.shape[0], KEY_CHUNK)]
    attn = _attend(q_ref[...], [k_ref[c, :] for c in chunks] + [ck_ref[0]],
                   [[v_ref[0, g, :, c] for g in range(N_KV_HEADS)] for c in chunks]
                   + [[cv_ref[0, g] for g in range(N_KV_HEADS)]])
    has_prev = (lax.rem(t, per_seq) > 0).astype(F32)
    has_next = (lax.rem(t, per_seq) < per_seq - 1).astype(F32)
    win = jnp.concatenate([cip_ref[...] * has_prev, ci_ref[...], cin_ref[...] * has_next], axis=0)
    conv = _conv_module(win, rows, cw_ref, cb_ref, lg_ref, lb_ref)
    sgu = _spatial_gate(u_ref[...], vn_ref, ws_ref, bs_ref, rows)
    o_ref[...] = _mix_out(x_ref[...], attn, conv, sgu, fn_ref[...], wo_ref, mod_ref, gn_ref)


def _back_ctx(x, mod, gn, q, kc, vc, ci, cw, cb, lg, lb, u, vn, ws, bs, gr, gi, cm, sm, wo,
              *, seq, layer):
    n = x.shape[0]
    tq = ROW_TILE
    per_tile = tq // seq
    row = lambda w: pl.BlockSpec((tq, w), lambda t: (t, 0))
    cache_spec = pl.BlockSpec((per_tile, 1, seq, KV_W), lambda t: (t, layer, 0, 0))
    in_specs = [row(D_MODEL), pl.BlockSpec((1, 6, D_MODEL), lambda t: (0, 0, 0)),
                _const_spec((4, D_MODEL)), row(ATTN_W), cache_spec, cache_spec, row(CONV_W),
                _const_spec(cw.shape), _const_spec((1, CONV_W)), _const_spec((1, CONV_W)),
                _const_spec((1, CONV_W)), row(SGU_W), row(SGU_W),
                _const_spec((SGU_HEADS, CHUNK, CHUNK)), _const_spec((CHUNK, SGU_W)),
                row(FNET_W), row(FNET_W), _const_spec((seq, seq)), _const_spec((seq, seq)),
                _const_spec((D_MODEL, D_MODEL))]
    return pl.pallas_call(
        functools.partial(_back_ctx_kernel, seq=seq),
        grid=(n // tq,),
        in_specs=in_specs,
        out_specs=row(D_MODEL),
        out_shape=jax.ShapeDtypeStruct(x.shape, F32),
        compiler_params=_cparams(1),
        name="back_ctx",
    )(x, mod, gn, q, kc, vc, ci, cw, cb, lg, lb, u, vn, ws, bs, gr, gi, cm, sm, wo)


def _back_lat(x, mod, gn, q, k, v, ck, cv, ci, cw, cb, lg, lb, u, vn, ws, bs, fn, wo, *, seq):
    n = x.shape[0]
    tq = LAT_TQ
    nt = n // tq
    per_seq = seq // tq
    nmod = mod.shape[0]
    past = ck.shape[1]
    halo_per = tq // CONV_PAD
    last_halo = n // CONV_PAD - 1
    row = lambda w: pl.BlockSpec((tq, w), lambda t: (t, 0))
    in_specs = [row(D_MODEL), pl.BlockSpec((1, 6, D_MODEL), lambda t: ((t * nmod) // nt, 0, 0)),
                _const_spec((4, D_MODEL)), row(ATTN_W),
                pl.BlockSpec((seq, KV_W), lambda t: (t // per_seq, 0)),
                pl.BlockSpec((1, N_KV_HEADS, VT_ROWS, seq), lambda t: (t // per_seq, 0, 0, 0)),
                pl.BlockSpec((1, past, KV_W), lambda t: (t // per_seq, 0, 0)),
                pl.BlockSpec((1, N_KV_HEADS, VT_ROWS, past), lambda t: (t // per_seq, 0, 0, 0)),
                row(CONV_W),
                pl.BlockSpec((CONV_PAD, CONV_W), lambda t: (jnp.maximum(t * halo_per - 1, 0), 0)),
                pl.BlockSpec((CONV_PAD, CONV_W),
                             lambda t: (jnp.minimum((t + 1) * halo_per, last_halo), 0)),
                _const_spec(cw.shape), _const_spec((1, CONV_W)), _const_spec((1, CONV_W)),
                _const_spec((1, CONV_W)), row(SGU_W), row(SGU_W),
                _const_spec((SGU_HEADS, CHUNK, CHUNK)), _const_spec((CHUNK, SGU_W)),
                row(FNET_W), _const_spec((D_MODEL, D_MODEL))]
    return pl.pallas_call(
        functools.partial(_back_lat_kernel, per_seq=per_seq),
        grid=(nt,),
        in_specs=in_specs,
        out_specs=row(D_MODEL),
        out_shape=jax.ShapeDtypeStruct(x.shape, F32),
        compiler_params=_cparams(1),
        name="back_lat",
    )(x, mod, gn, q, k, v, ck, cv, ci, ci, ci, cw, cb, lg, lb, u, vn, ws, bs, fn, wo)


def _ffn_kernel(x_ref, xp_ref, xn_ref, mod_ref, gn_ref, wup_ref, cw_ref, cb_ref, wdn_ref, o_ref,
                xs_ref, act_ref, os_ref, *, seq):
    t = pl.program_id(0)
    tm = x_ref.shape[0]
    ext = tm + 2 * SUBLANES
    seg = ext // SUBLANES
    nslab = D_MODEL // LANES
    m = mod_ref[0]
    sh2, sc2, gt2 = m[3:4], m[4:5], m[5:6]
    for c in range(nslab):
        ls = slice(c * LANES, (c + 1) * LANES)
        xs_ref[c, 0:SUBLANES, :] = xp_ref[:, ls]
        xs_ref[c, SUBLANES:SUBLANES + tm, :] = x_ref[:, ls]
        xs_ref[c, SUBLANES + tm:ext, :] = xn_ref[:, ls]
    xq = jnp.concatenate(
        [jnp.concatenate([xs_ref[c, pl.ds(i, SUBLANES, stride=seg), :] for i in range(seg)], axis=0)
         for c in range(nslab)], axis=1)
    he = _rms(xq, gn_ref[2:3]) * (1.0 + sc2) + sh2
    p = lax.broadcasted_iota(jnp.int32, (ext, D_MODEL), 0)
    tile_row = (p & (SUBLANES - 1)) * seg + (p >> 3)
    lo = jnp.where(lax.rem(t * tm, seq) == 0, SUBLANES, 0)
    hi = jnp.where(lax.rem((t + 1) * tm, seq) == 0, tm + SUBLANES, ext)
    hb = jnp.where((tile_row >= lo) & (tile_row < hi), he, 0.0).astype(BF16)
    starts = [SUBLANES + r for r in range(seq, tm, seq)]
    assert all(r % seg == 0 for r in starts)
    sub = lax.broadcasted_iota(jnp.int32, (SUBLANES, FFN_CHUNK), 0)

    def conv3(up, col0):
        w = cw_ref[:, col0:col0 + FFN_CHUNK]
        wrap_prev = pltpu.roll(up[ext - SUBLANES:ext], 1, 0)
        wrap_next = pltpu.roll(up[0:SUBLANES], SUBLANES - 1, 0)
        for r in starts:
            wrap_prev = jnp.where(sub == r // seg, 0.0, wrap_prev)
            wrap_next = jnp.where(sub == r // seg - 1, 0.0, wrap_next)
        prev = jnp.concatenate([wrap_prev, up[0:ext - SUBLANES]], axis=0)
        nxt = jnp.concatenate([up[SUBLANES:ext], wrap_next], axis=0)
        return w[0:1] * prev + w[1:2] * up + w[2:3] * nxt + cb_ref[:, col0:col0 + FFN_CHUNK]

    for c in range(D_FF // FFN_CHUNK):
        ca = c * FFN_CHUNK
        cg = D_FF + c * FFN_CHUNK
        a = conv3(_dot(hb, wup_ref[:, ca:ca + FFN_CHUNK]), ca)
        g = conv3(_dot(hb, wup_ref[:, cg:cg + FFN_CHUNK]), cg)
        act_ref[:, ca:ca + FFN_CHUNK] = (a * _sigmoid(a) * g).astype(BF16)
    y = xq + gt2 * _rms(_dot(act_ref[...], wdn_ref[...]), gn_ref[3:4])
    for c in range(nslab):
        for i in range(seg):
            os_ref[c, pl.ds(i, SUBLANES, stride=seg), :] = (
                y[i * SUBLANES:(i + 1) * SUBLANES, c * LANES:(c + 1) * LANES])
    for c in range(nslab):
        o_ref[:, c * LANES:(c + 1) * LANES] = os_ref[c, SUBLANES:SUBLANES + tm, :]


def _ffn(x, mod, gn, w_up, cw, cb, w_dn, *, seq):
    n = x.shape[0]
    tm = ROW_TILE
    nt = n // tm
    nmod = mod.shape[0]
    per8 = tm // SUBLANES
    last8 = n // SUBLANES - 1
    ext = tm + 2 * SUBLANES
    slab = pltpu.VMEM((D_MODEL // LANES, ext, LANES), F32)
    return pl.pallas_call(
        functools.partial(_ffn_kernel, seq=seq),
        grid=(nt,),
        in_specs=[pl.BlockSpec((tm, D_MODEL), lambda t: (t, 0)),
                  pl.BlockSpec((SUBLANES, D_MODEL), lambda t: (jnp.maximum(t * per8 - 1, 0), 0)),
                  pl.BlockSpec((SUBLANES, D_MODEL),
                               lambda t: (jnp.minimum((t + 1) * per8, last8), 0)),
                  pl.BlockSpec((1, 6, D_MODEL), lambda t: ((t * nmod) // nt, 0, 0)),
                  _const_spec((4, D_MODEL)),
                  _const_spec((D_MODEL, 2 * D_FF)),
                  _const_spec((SUBLANES, 2 * D_FF)), _const_spec((1, 2 * D_FF)),
                  _const_spec((D_FF, D_MODEL))],
        out_specs=pl.BlockSpec((tm, D_MODEL), lambda t: (t, 0)),
        out_shape=jax.ShapeDtypeStruct(x.shape, F32),
        scratch_shapes=[slab, pltpu.VMEM((ext, D_FF), BF16), slab],
        compiler_params=_cparams(1),
        name="ffn",
    )(x, x, x, mod, gn, w_up, cw, cb, w_dn)


def _np_consts(ctx_seq):
    hd = np.arange(ATTN_W) // HEAD_DIM
    pmat = (hd[:, None] == hd[None, :]).astype(np.float32) / HEAD_DIM
    ch = np.arange(FNET_W)
    same = (ch[:, None] // FNET_GROUP_W) == (ch[None, :] // FNET_GROUP_W)
    ang = 2.0 * np.pi * ((ch[:, None] % FNET_GROUP_W) * (ch[None, :] % FNET_GROUP_W)) / FNET_GROUP_W
    cc = np.where(same, np.cos(ang), 0.0).astype(np.float32)
    cs = np.where(same, np.sin(ang), 0.0).astype(np.float32)
    n = np.arange(ctx_seq)
    ang = 2.0 * np.pi * ((n[:, None] * n[None, :]) % ctx_seq) / ctx_seq
    c_ctx = np.cos(ang).astype(np.float32)
    s_ctx = np.sin(ang).astype(np.float32)
    r = np.arange(FFT_RADIX)
    ang = 2.0 * np.pi * ((r[:, None] * r[None, :]) % FFT_RADIX) / FFT_RADIX
    c64, s64 = np.cos(ang), np.sin(ang)
    eye = np.eye(FFT_BLK)
    kc, ks = np.kron(c64, eye), np.kron(s64, eye)
    k1 = np.block([[kc, ks], [-ks, kc]]).astype(np.float32)
    n_seq = FFT_RADIX * FFT_RADIX
    ang = 2.0 * np.pi * (r[:, None] * r[None, :]) / n_seq
    tc = np.repeat(np.cos(ang)[:, :, None], LANES, axis=2).astype(np.float32)
    ts = np.repeat(np.sin(ang)[:, :, None], LANES, axis=2).astype(np.float32)
    scale = 1.0 / np.sqrt(n_seq * FNET_GROUP_W)
    k2c = np.einsum("db,ec->decb", c64, eye).reshape(FFT_RADIX * FFT_BLK, FFT_BLK * FFT_RADIX)
    k2s = np.einsum("db,ec->decb", s64, eye).reshape(FFT_RADIX * FFT_BLK, FFT_BLK * FFT_RADIX)
    k2 = (np.concatenate([k2c, k2s], axis=1) * scale).astype(np.float32)
    return pmat, cc, cs, c_ctx, s_ctx, k1, k2, tc, ts


def _rope_tables(s):
    rows = s // GRID_W
    row = jnp.repeat(jnp.arange(rows, dtype=F32), GRID_W)
    col = jnp.tile(jnp.arange(GRID_W, dtype=F32), rows)
    n_f = HEAD_DIM // 4
    inv = ROPE_THETA ** (-jnp.arange(n_f, dtype=F32) / n_f)
    ang = jnp.concatenate([row[:, None] * inv, col[:, None] * inv], axis=-1)
    cos = jnp.repeat(jnp.cos(ang), 2, axis=-1)
    sin = jnp.repeat(jnp.sin(ang), 2, axis=-1)
    sign = jnp.tile(jnp.array([-1.0, 1.0], F32), HEAD_DIM // 2)
    return jnp.tile(cos, (1, N_Q_HEADS)), jnp.tile(sin * sign, (1, N_Q_HEADS))


def kernel(x_prompt, x_sample, cache_k, cache_v, c, c_ctx, w_ada, b_ada, g_norm, w_in, g_q, g_k, conv_w, conv_b, conv_ln_g, conv_ln_b, sgu_g, w_s, b_s, w_out, w_up, ffn_conv_w, ffn_conv_b, w_down):
    bsz, seq, _ = x_prompt.shape
    dec_b, dec_s, _ = x_sample.shape
    past = cache_k.shape[2]
    assert dec_s == FFT_RADIX * FFT_RADIX and ROW_TILE % seq == 0 and dec_s % ROW_TILE == 0

    consts = _np_consts(seq)
    pmat, cc, cs, c_ctx_m, s_ctx_m, k1, k2 = (jnp.asarray(a).astype(BF16) for a in consts[:7])
    tc, ts = jnp.asarray(consts[7]), jnp.asarray(consts[8])
    rope = _rope_tables(dec_s)

    cond8 = jnp.zeros((SUBLANES, D_MODEL), F32).at[0].set(c_ctx).at[1:1 + dec_b].set(c)
    mod = _ada(cond8, w_ada, b_ada).reshape(DEPTH, SUBLANES, 6, D_MODEL)

    q_cols = np.concatenate([np.arange(h * HEAD_DIM, (h + 1) * HEAD_DIM) for h in (0, 2, 1, 3)])
    in_cols = np.concatenate([q_cols, np.arange(ATTN_W, IN_COLS)])
    ws_b = w_s.astype(BF16)
    cw_pad = jnp.pad(conv_w, ((0, 0), (0, 32 - CONV_K), (0, 0)))
    fcw_pad = jnp.pad(ffn_conv_w, ((0, 0), (0, SUBLANES - FFN_CONV_K), (0, 0)))
    bs_full = jnp.repeat(jnp.swapaxes(b_s, 1, 2), SGU_HEAD_W, axis=2)
    ck = cache_k.reshape(dec_b, DEPTH, past, KV_W).astype(BF16)
    cvt = jnp.transpose(cache_v, (0, 1, 3, 4, 2))
    ones_row = jnp.zeros((dec_b, DEPTH, N_KV_HEADS, VT_ROWS - HEAD_DIM, past), F32).at[:, :, :, 0].set(1.0)
    cv = jnp.concatenate([cvt, ones_row], axis=3).astype(BF16)
    row1 = lambda a, l: a[l][None, :]

    xp = x_prompt.reshape(bsz * seq, D_MODEL)
    xs = x_sample.reshape(dec_b * dec_s, D_MODEL)
    kv_ctx = None
    for l in range(DEPTH):
        gq = jnp.tile(g_q[l], N_Q_HEADS)[None, :]
        gk = jnp.tile(g_k[l], N_KV_HEADS)[None, :]
        w_in_l = w_in[l][:, in_cols].astype(BF16)
        w_out_l = w_out[l].astype(BF16)
        common = (g_norm[l], w_in_l, gq, gk, pmat, row1(sgu_g, l), cc, cs)
        conv_args = (cw_pad[l], row1(conv_b, l), row1(conv_ln_g, l), row1(conv_ln_b, l))
        ffn_args = (g_norm[l], w_up[l].astype(BF16), fcw_pad[l], row1(ffn_conv_b, l),
                    w_down[l].astype(BF16))

        mod_c = mod[l, 0:1]
        q, kc, vc, ci, u, vn, gr, gi = _front(xp, mod_c, *common, None, kv_ctx,
                                              latent=False, seq=seq, layer=l)
        kv_ctx = (kc, vc)
        xp = _back_ctx(xp, mod_c, g_norm[l], q, kc, vc, ci, *conv_args, u, vn, ws_b[l], bs_full[l],
                       gr, gi, c_ctx_m, s_ctx_m, w_out_l, seq=seq, layer=l)
        xp = _ffn(xp, mod_c, *ffn_args, seq=seq)

        mod_s = mod[l, 1:1 + dec_b]
        q, k, v, ci, u, vn, gr, gi = _front(xs, mod_s, *common, rope, None,
                                             latent=True, seq=dec_s, layer=l)
        grid4 = (dec_b, FFT_RADIX, FFT_RADIX, FNET_W)
        fn = _fft(gr.reshape(grid4), gi.reshape(grid4), k1, tc, ts, k2).reshape(-1, FNET_W)
        xs = _back_lat(xs, mod_s, g_norm[l], q, k, v, ck[:, l], cv[:, l], ci, *conv_args, u, vn,
                       ws_b[l], bs_full[l], fn, w_out_l, seq=dec_s)
        xs = _ffn(xs, mod_s, *ffn_args, seq=dec_s)

    new_k = kv_ctx[0].reshape(bsz, DEPTH, seq, N_KV_HEADS, HEAD_DIM)
    new_v = kv_ctx[1].reshape(bsz, DEPTH, seq, N_KV_HEADS, HEAD_DIM)
    return (xp.reshape(x_prompt.shape), xs.reshape(x_sample.shape), new_k, new_v)
```

```python
import functools

import numpy as np
import jax
import jax.numpy as jnp
from jax import lax
from jax.experimental import pallas as pl
from jax.experimental.pallas import tpu as pltpu

D_MODEL = 1024
DEPTH = 2
GRID_W = 64
HEAD_DIM = 64
N_Q_HEADS = 4
N_KV_HEADS = 2
Q_PER_KV = N_Q_HEADS // N_KV_HEADS
ATTN_W = N_Q_HEADS * HEAD_DIM
KV_W = N_KV_HEADS * HEAD_DIM
ATTN_SCALE = HEAD_DIM ** -0.5
Q_SCALE = ATTN_SCALE * float(np.log2(np.e))
ROPE_THETA = 10000.0
CONV_W = 256
CONV_K = 31
CONV_PAD = 16
CONV_TAP_ROWS = 32
SGU_W = 256
SGU_HEADS = 4
SGU_HEAD_W = SGU_W // SGU_HEADS
CHUNK = 128
FNET_W = 256
FNET_GROUPS = 4
FNET_GROUP_W = FNET_W // FNET_GROUPS
D_FF = 2816
FFN_CONV_K = 3
EPS = 1e-6
SPLIT_Q = ATTN_W
SPLIT_K = SPLIT_Q + KV_W
SPLIT_V = SPLIT_K + KV_W
SPLIT_CONV = SPLIT_V + 2 * CONV_W
SPLIT_SGU = SPLIT_CONV + 2 * SGU_W
IN_COLS = SPLIT_SGU + FNET_W
Q_HEAD_ORDER = (0, 2, 1, 3)

SUBLANES = 8
LANES = 128
ROW_TILE = 512
LAT_TQ = 256
VT_ROWS = 80
KEY_CHUNK = 512
ATTN_LOOKAHEAD = 3
FFN_CHUNK = 256
FFT_RADIX = 64
FFT_BLK = 8
VMEM_LIMIT = 56 * 1024 * 1024

F32 = jnp.float32
BF16 = jnp.bfloat16


def _cparams(n_axes):
    return pltpu.CompilerParams(dimension_semantics=("arbitrary",) * n_axes,
                                vmem_limit_bytes=VMEM_LIMIT)


def _const_spec(shape):
    nd = len(shape)
    return pl.BlockSpec(shape, lambda *_: (0,) * nd, pipeline_mode=pl.Buffered(1))


def _layer_spec(shape, layer):
    nd = len(shape)
    return pl.BlockSpec((1,) + tuple(shape), lambda *_: (layer,) + (0,) * nd,
                        pipeline_mode=pl.Buffered(1))


def _mod_spec(layer, row0, nrows, nt):
    return pl.BlockSpec((1, 1, 6, D_MODEL), lambda t: (layer, row0 + (t * nrows) // nt, 0, 0))


def _dot(a, b):
    return jnp.dot(a, b, preferred_element_type=F32)


def _dot_nt(a, b):
    return lax.dot_general(a, b, (((1,), (1,)), ((), ())), preferred_element_type=F32)


def _sigmoid(x):
    return 1.0 / (1.0 + jnp.exp(-x))


def _rms(x, g):
    ms = jnp.mean(x * x, axis=-1, keepdims=True)
    return x * lax.rsqrt(ms + EPS) * g


def _head_rms(z, pmat, g):
    sq = z * z
    hi = sq.astype(BF16)
    lo = (sq - hi.astype(F32)).astype(BF16)
    ms = _dot(hi, pmat) + _dot(lo, pmat)
    return z * lax.rsqrt(ms + EPS) * g


def _rope(x, cos, sin_signed):
    w = x.shape[1]
    nxt = pltpu.roll(x, w - 1, 1)
    prv = pltpu.roll(x, 1, 1)
    lane = lax.broadcasted_iota(jnp.int32, x.shape, 1)
    swapped = jnp.where((lane & 1) == 0, nxt, prv)
    return x * cos + swapped * sin_signed


def _ada_kernel(cond_ref, w_ref, b_ref, o_ref):
    c = cond_ref[...]
    s = (c * _sigmoid(c)).astype(BF16)
    o_ref[0] = _dot(s, w_ref[0].astype(BF16)) + b_ref[0]


def _ada(cond8, w_ada, b_ada):
    tn = 1536
    n_out = 6 * D_MODEL
    return pl.pallas_call(
        _ada_kernel,
        grid=(DEPTH, n_out // tn),
        in_specs=[pl.BlockSpec((SUBLANES, D_MODEL), lambda l, j: (0, 0)),
                  pl.BlockSpec((1, D_MODEL, tn), lambda l, j: (l, 0, j)),
                  pl.BlockSpec((1, 1, tn), lambda l, j: (l, 0, j))],
        out_specs=pl.BlockSpec((1, SUBLANES, tn), lambda l, j: (l, 0, j)),
        out_shape=jax.ShapeDtypeStruct((DEPTH, SUBLANES, n_out), F32),
        compiler_params=_cparams(2),
        name="ada",
    )(cond8, w_ada, b_ada.reshape(DEPTH, 1, n_out))


def _front_kernel(*refs, latent):
    (x_ref, mod_ref, gn_ref, win_ref, gq_ref, gk_ref, pm_ref, sg_ref, cc_ref, cs_ref) = refs[:10]
    refs = refs[10:]
    if latent:
        cos_ref, sin_ref = refs[:2]
        refs = refs[2:]
    q_ref, k_ref, v_ref, ci_ref, u_ref, vn_ref, gr_ref, gi_ref, wb_ref = refs

    @pl.when(pl.program_id(0) == 0)
    def _():
        wq = win_ref[0, :, :SPLIT_Q]
        wb_ref[:, :SPLIT_Q] = jnp.concatenate(
            [wq[:, h * HEAD_DIM:(h + 1) * HEAD_DIM] for h in Q_HEAD_ORDER], axis=1).astype(BF16)
        wb_ref[:, SPLIT_Q:] = win_ref[0, :, SPLIT_Q:].astype(BF16)

    m = mod_ref[0, 0]
    sh1, sc1 = m[0:1], m[1:2]
    h = _rms(x_ref[...], gn_ref[0, 0:1]) * (1.0 + sc1) + sh1
    z = _dot(h.astype(BF16), wb_ref[...])
    pm = pm_ref[...]
    q = _head_rms(z[:, :SPLIT_Q], pm, gq_ref[0])
    k = _head_rms(z[:, SPLIT_Q:SPLIT_K], pm[:KV_W, :KV_W], gk_ref[0])
    v = z[:, SPLIT_K:SPLIT_V]
    if latent:
        cos = cos_ref[...]
        sin = sin_ref[...]
        q = _rope(q, jnp.concatenate([cos, cos], axis=1), jnp.concatenate([sin, sin], axis=1))
        k = _rope(k, cos, sin)
        for g, ext in enumerate(_vt_ext(v)):
            v_ref[0, g] = ext
    else:
        v_ref[...] = v
    k_ref[...] = k.astype(k_ref.dtype)
    q_ref[...] = (q * Q_SCALE).astype(q_ref.dtype)
    a = z[:, SPLIT_V:SPLIT_V + CONV_W]
    gt = z[:, SPLIT_V + CONV_W:SPLIT_CONV]
    ci_ref[...] = a * _sigmoid(gt)
    u_ref[...] = z[:, SPLIT_CONV:SPLIT_CONV + SGU_W]
    vn_ref[...] = _rms(z[:, SPLIT_CONV + SGU_W:SPLIT_SGU], sg_ref[0]).astype(vn_ref.dtype)
    ff = z[:, SPLIT_SGU:].astype(BF16)
    gr_ref[...] = _dot(ff, cc_ref[...]).astype(gr_ref.dtype)
    gi_ref[...] = (-_dot(ff, cs_ref[...])).astype(gi_ref.dtype)


def _front(x, mod, gn, w_in, gq, gk, pm, sgu_g, cc, cs, rope, *, latent, seq, layer):
    n = x.shape[0]
    tm = ROW_TILE
    nt = n // tm
    nb = n // seq
    row = lambda w: pl.BlockSpec((tm, w), lambda t: (t, 0))
    in_specs = [row(D_MODEL),
                _mod_spec(layer, 1, nb, nt) if latent else _mod_spec(layer, 0, 1, nt),
                _layer_spec((4, D_MODEL), layer),
                _layer_spec((D_MODEL, IN_COLS), layer),
                _layer_spec((1, ATTN_W), layer), _layer_spec((1, KV_W), layer),
                _const_spec((ATTN_W, ATTN_W)), _layer_spec((1, SGU_W), layer),
                _const_spec((FNET_W, FNET_W)), _const_spec((FNET_W, FNET_W))]
    args = [x, mod, gn, w_in, gq, gk, pm, sgu_g, cc, cs]
    if latent:
        per_seq = seq // tm
        in_specs += [pl.BlockSpec((tm, KV_W), lambda t: (t % per_seq, 0))] * 2
        args += list(rope)
        kv_specs = [row(KV_W), pl.BlockSpec((1, N_KV_HEADS, VT_ROWS, tm),
                                            lambda t: (t // per_seq, 0, 0, t % per_seq))]
        kv_shapes = [jax.ShapeDtypeStruct((n, KV_W), BF16),
                     jax.ShapeDtypeStruct((nb, N_KV_HEADS, VT_ROWS, seq), BF16)]
        f_dt = F32
    else:
        kv_specs = [row(KV_W), row(KV_W)]
        kv_shapes = [jax.ShapeDtypeStruct((n, KV_W), F32)] * 2
        f_dt = BF16
    outs = [(ATTN_W, BF16), None, None, (CONV_W, F32), (SGU_W, F32), (SGU_W, BF16),
            (FNET_W, f_dt), (FNET_W, f_dt)]
    out_specs = [row(o[0]) if o else None for o in outs]
    out_shape = [jax.ShapeDtypeStruct((n, o[0]), o[1]) if o else None for o in outs]
    out_specs[1:3] = kv_specs
    out_shape[1:3] = kv_shapes
    return pl.pallas_call(
        functools.partial(_front_kernel, latent=latent),
        grid=(nt,),
        in_specs=in_specs,
        out_specs=out_specs,
        out_shape=out_shape,
        scratch_shapes=[pltpu.VMEM((D_MODEL, IN_COLS), BF16)],
        compiler_params=_cparams(1),
        name="front_lat" if latent else "front_ctx",
    )(*args)


def _fft_kernel(gr_ref, gi_ref, k1_ref, tc_ref, ts_ref, k2_ref, o_ref, yr_ref, yi_ref):
    j = pl.program_id(1)
    nblk = FFT_RADIX // FFT_BLK
    rows = FFT_RADIX * FFT_BLK

    @pl.when(j < nblk)
    def _():
        g = jnp.concatenate([gr_ref[0].reshape(rows, FNET_W), gi_ref[0].reshape(rows, FNET_W)],
                            axis=0).astype(BF16)
        y = _dot(k1_ref[...], g)
        yr, yi = y[:rows], y[rows:]
        tc = tc_ref[...].reshape(rows, LANES)
        ts = ts_ref[...].reshape(rows, LANES)
        tc = jnp.concatenate([tc] * (FNET_W // LANES), axis=1)
        ts = jnp.concatenate([ts] * (FNET_W // LANES), axis=1)
        col = pl.ds(pl.multiple_of(j * FFT_BLK, FFT_BLK), FFT_BLK)
        yr_ref[:, col, :] = (yr * tc + yi * ts).reshape(FFT_RADIX, FFT_BLK, FNET_W)
        yi_ref[:, col, :] = (yi * tc - yr * ts).reshape(FFT_RADIX, FFT_BLK, FNET_W)

    @pl.when(j >= nblk)
    def _():
        blk = pl.ds(pl.multiple_of((j - nblk) * FFT_BLK, FFT_BLK), FFT_BLK)
        y = jnp.concatenate([yr_ref[blk].reshape(rows, FNET_W), yi_ref[blk].reshape(rows, FNET_W)],
                            axis=0).astype(BF16)
        o_ref[0] = _dot(k2_ref[...], y).reshape(FFT_RADIX, FFT_BLK, FNET_W)


def _fft(gr, gi, k1, tc, ts, k2):
    nb = gr.shape[0]
    nblk = FFT_RADIX // FFT_BLK
    rows = FFT_RADIX * FFT_BLK
    g_spec = pl.BlockSpec((1, FFT_RADIX, FFT_BLK, FNET_W),
                          lambda i, j: (i, 0, jnp.minimum(j, nblk - 1), 0))
    t_spec = pl.BlockSpec((FFT_RADIX, FFT_BLK, LANES), lambda i, j: (0, jnp.minimum(j, nblk - 1), 0))
    return pl.pallas_call(
        _fft_kernel,
        grid=(nb, 2 * nblk),
        in_specs=[g_spec, g_spec, _const_spec((2 * rows, 2 * rows)), t_spec, t_spec,
                  _const_spec((rows, 2 * rows))],
        out_specs=pl.BlockSpec((1, FFT_RADIX, FFT_BLK, FNET_W),
                               lambda i, j: (i, 0, jnp.maximum(j - nblk, 0), 0)),
        out_shape=jax.ShapeDtypeStruct(gr.shape, F32),
        scratch_shapes=[pltpu.VMEM((FFT_RADIX, FFT_RADIX, FNET_W), F32)] * 2,
        compiler_params=_cparams(2),
        name="fft",
    )(gr, gi, k1, tc, ts, k2)


def _vt_ext(v):
    vt = v.T
    n = v.shape[0]
    pad = jnp.where(lax.broadcasted_iota(jnp.int32, (VT_ROWS - HEAD_DIM, n), 0) == 0, 1.0, 0.0)
    return [jnp.concatenate([vt[g * HEAD_DIM:(g + 1) * HEAD_DIM], pad], axis=0).astype(BF16)
            for g in range(N_KV_HEADS)]


def _attend(q, k_chunks, vt_chunks):
    lane = lax.broadcasted_iota(jnp.int32, (q.shape[0], KV_W), 1)
    qm = {}
    for half in range(Q_PER_KV):
        q2 = q[:, half * KV_W:(half + 1) * KV_W]
        for g in range(N_KV_HEADS):
            in_head = (lane >= g * HEAD_DIM) & (lane < (g + 1) * HEAD_DIM)
            qm[half, g] = jnp.where(in_head, q2, jnp.zeros_like(q2))
    items = [(c, hg) for c in range(len(k_chunks)) for hg in qm]
    state = {}

    def consume(c, hg, s):
        vt = vt_chunks[c][hg[1]]
        cm = jnp.max(s, axis=0, keepdims=True)
        if hg not in state:
            state[hg] = (cm, _dot(vt, jnp.exp2(s - cm).astype(BF16)))
        else:
            m, acc = state[hg]
            m_new = jnp.maximum(m, cm)
            state[hg] = (m_new, acc * jnp.exp2(m - m_new)
                         + _dot(vt, jnp.exp2(s - m_new).astype(BF16)))

    pending = []
    for c, hg in items:
        pending.append((c, hg, _dot_nt(k_chunks[c], qm[hg])))
        if len(pending) > ATTN_LOOKAHEAD:
            consume(*pending.pop(0))
    for item in pending:
        consume(*item)
    outs = {}
    for (half, g), (_, acc) in state.items():
        outs[Q_HEAD_ORDER[Q_PER_KV * half + g]] = acc[:HEAD_DIM] / acc[HEAD_DIM:HEAD_DIM + 1]
    return jnp.concatenate([outs[h] for h in range(N_Q_HEADS)], axis=0).T


def _conv_module(win, rows, cw_ref, cb_ref, lg_ref, lb_ref):
    win_rows = rows + 2 * CONV_PAD
    shifted = [win] + [pltpu.roll(win, win_rows - sft, 0) for sft in range(1, SUBLANES)]
    cw = cw_ref[0]
    conv = jnp.zeros((rows, CONV_W), F32) + cb_ref[0]
    for j in range(CONV_K):
        off = j + CONV_PAD - CONV_K // 2
        base = (off // SUBLANES) * SUBLANES
        conv = conv + cw[j:j + 1] * shifted[off % SUBLANES][base:base + rows]
    mu = jnp.mean(conv, axis=-1, keepdims=True)
    cen = conv - mu
    var = jnp.mean(cen * cen, axis=-1, keepdims=True)
    conv = cen * lax.rsqrt(var + EPS) * lg_ref[0] + lb_ref[0]
    return conv * _sigmoid(conv)


def _spatial_gate(u, vn_ref, ws_ref, bs_ref, rows):
    lane = lax.broadcasted_iota(jnp.int32, (CHUNK, SGU_W), 1)
    mixed_chunks = []
    for c in range(rows // CHUNK):
        vn_c = vn_ref[c * CHUNK:(c + 1) * CHUNK, :]
        mixed = jnp.zeros((CHUNK, SGU_W), F32)
        for hd in range(SGU_HEADS):
            full = _dot(ws_ref[0, hd], vn_c)
            in_head = (lane >= hd * SGU_HEAD_W) & (lane < (hd + 1) * SGU_HEAD_W)
            mixed = jnp.where(in_head, full, mixed)
        mixed_chunks.append(mixed + bs_ref[0])
    return u * jnp.concatenate(mixed_chunks, axis=0)


def _load_wo(wo_ref, wob_ref):
    @pl.when(pl.program_id(0) == 0)
    def _():
        wob_ref[...] = wo_ref[0].astype(BF16)


def _mix_out(x, attn, conv, sgu, fnet, wob_ref, mod_ref, gn_ref):
    mix = (_dot(attn.astype(BF16), wob_ref[0:ATTN_W, :])
           + _dot(conv.astype(BF16), wob_ref[ATTN_W:ATTN_W + CONV_W, :])
           + _dot(sgu.astype(BF16), wob_ref[ATTN_W + CONV_W:ATTN_W + CONV_W + SGU_W, :])
           + _dot(fnet.astype(BF16), wob_ref[ATTN_W + CONV_W + SGU_W:, :]))
    gt1 = mod_ref[0, 0][2:3]
    return x + gt1 * _rms(mix, gn_ref[0, 1:2])


def _back_ctx_kernel(x_ref, mod_ref, gn_ref, q_ref, k_ref, v_ref, ci_ref, cw_ref, cb_ref,
                     lg_ref, lb_ref, u_ref, vn_ref, ws_ref, bs_ref, gr_ref, gi_ref,
                     c_ref, s_ref, wo_ref, o_ref, wob_ref, *, seq):
    _load_wo(wo_ref, wob_ref)
    rows = x_ref.shape[0]
    halo = jnp.zeros((CONV_PAD, CONV_W), F32)
    attn, conv, fnet = [], [], []
    for i in range(rows // seq):
        rs = slice(i * seq, (i + 1) * seq)
        attn.append(_attend(q_ref[rs, :], [k_ref[rs, :].astype(BF16)], [_vt_ext(v_ref[rs, :])]))
        win = jnp.concatenate([halo, ci_ref[rs, :], halo], axis=0)
        conv.append(_conv_module(win, seq, cw_ref, cb_ref, lg_ref, lb_ref))
        fnet.append((_dot(c_ref[...], gr_ref[rs, :]) + _dot(s_ref[...], gi_ref[rs, :]))
                    * (1.0 / np.sqrt(seq * FNET_GROUP_W)))
    cat = lambda parts: jnp.concatenate(parts, axis=0)
    sgu = _spatial_gate(u_ref[...], vn_ref, ws_ref, bs_ref, rows)
    o_ref[...] = _mix_out(x_ref[...], cat(attn), cat(conv), sgu, cat(fnet), wob_ref, mod_ref, gn_ref)


def _back_lat_kernel(x_ref, mod_ref, gn_ref, q_ref, k_ref, v_ref, ck_ref, cv_ref,
                     ci_ref, cip_ref, cin_ref, cw_ref, cb_ref, lg_ref, lb_ref,
                     u_ref, vn_ref, ws_ref, bs_ref, fn_ref, wo_ref, o_ref, wob_ref, *, per_seq):
    _load_wo(wo_ref, wob_ref)
    rows = x_ref.shape[0]
    t = pl.program_id(0)
    chunks = [slice(c, c + KEY_CHUNK) for c in range(0, k_ref.shape[0], KEY_CHUNK)]
    attn = _attend(q_ref[...], [k_ref[c, :] for c in chunks] + [ck_ref[0, 0]],
                   [[v_ref[0, g, :, c] for g in range(N_KV_HEADS)] for c in chunks]
                   + [[cv_ref[0, 0, g] for g in range(N_KV_HEADS)]])
    has_prev = (lax.rem(t, per_seq) > 0).astype(F32)
    has_next = (lax.rem(t, per_seq) < per_seq - 1).astype(F32)
    win = jnp.concatenate([cip_ref[...] * has_prev, ci_ref[...], cin_ref[...] * has_next], axis=0)
    conv = _conv_module(win, rows, cw_ref, cb_ref, lg_ref, lb_ref)
    sgu = _spatial_gate(u_ref[...], vn_ref, ws_ref, bs_ref, rows)
    o_ref[...] = _mix_out(x_ref[...], attn, conv, sgu, fn_ref[...], wob_ref, mod_ref, gn_ref)


def _mixer_param_specs(layer):
    vec = _layer_spec((1, CONV_W), layer)
    return ([_layer_spec((CONV_TAP_ROWS, CONV_W), layer), vec, vec, vec],
            [_layer_spec((SGU_HEADS, CHUNK, CHUNK), layer), _layer_spec((CHUNK, SGU_W), layer)])


def _back_ctx(x, mod, gn, q, k, v, ci, cw, cb, lg, lb, u, vn, ws, bs, gr, gi, cm, sm, wo,
              *, seq, layer):
    n = x.shape[0]
    tq = ROW_TILE
    row = lambda w: pl.BlockSpec((tq, w), lambda t: (t, 0))
    conv_specs, sgu_specs = _mixer_param_specs(layer)
    in_specs = ([row(D_MODEL), _mod_spec(layer, 0, 1, n // tq), _layer_spec((4, D_MODEL), layer),
                 row(ATTN_W), row(KV_W), row(KV_W), row(CONV_W)] + conv_specs
                + [row(SGU_W), row(SGU_W)] + sgu_specs
                + [row(FNET_W), row(FNET_W), _const_spec((seq, seq)), _const_spec((seq, seq)),
                   _layer_spec((D_MODEL, D_MODEL), layer)])
    return pl.pallas_call(
        functools.partial(_back_ctx_kernel, seq=seq),
        grid=(n // tq,),
        in_specs=in_specs,
        out_specs=row(D_MODEL),
        out_shape=jax.ShapeDtypeStruct(x.shape, F32),
        scratch_shapes=[pltpu.VMEM((D_MODEL, D_MODEL), BF16)],
        compiler_params=_cparams(1),
        name="back_ctx",
    )(x, mod, gn, q, k, v, ci, cw, cb, lg, lb, u, vn, ws, bs, gr, gi, cm, sm, wo)


def _back_lat(x, mod, gn, q, k, v, ck, cv, ci, cw, cb, lg, lb, u, vn, ws, bs, fn, wo,
              *, seq, layer):
    n = x.shape[0]
    tq = LAT_TQ
    nt = n // tq
    per_seq = seq // tq
    past = ck.shape[2]
    halo_per = tq // CONV_PAD
    last_halo = n // CONV_PAD - 1
    row = lambda w: pl.BlockSpec((tq, w), lambda t: (t, 0))
    conv_specs, sgu_specs = _mixer_param_specs(layer)
    in_specs = ([row(D_MODEL), _mod_spec(layer, 1, n // seq, nt), _layer_spec((4, D_MODEL), layer),
                 row(ATTN_W),
                 pl.BlockSpec((seq, KV_W), lambda t: (t // per_seq, 0)),
                 pl.BlockSpec((1, N_KV_HEADS, VT_ROWS, seq), lambda t: (t // per_seq, 0, 0, 0)),
                 pl.BlockSpec((1, 1, past, KV_W), lambda t: (t // per_seq, layer, 0, 0)),
                 pl.BlockSpec((1, 1, N_KV_HEADS, VT_ROWS, past),
                              lambda t: (t // per_seq, layer, 0, 0, 0)),
                 row(CONV_W),
                 pl.BlockSpec((CONV_PAD, CONV_W), lambda t: (jnp.maximum(t * halo_per - 1, 0), 0)),
                 pl.BlockSpec((CONV_PAD, CONV_W),
                              lambda t: (jnp.minimum((t + 1) * halo_per, last_halo), 0))]
                + conv_specs + [row(SGU_W), row(SGU_W)] + sgu_specs
                + [row(FNET_W), _layer_spec((D_MODEL, D_MODEL), layer)])
    return pl.pallas_call(
        functools.partial(_back_lat_kernel, per_seq=per_seq),
        grid=(nt,),
        in_specs=in_specs,
        out_specs=row(D_MODEL),
        out_shape=jax.ShapeDtypeStruct(x.shape, F32),
        scratch_shapes=[pltpu.VMEM((D_MODEL, D_MODEL), BF16)],
        compiler_params=_cparams(1),
        name="back_lat",
    )(x, mod, gn, q, k, v, ck, cv, ci, ci, ci, cw, cb, lg, lb, u, vn, ws, bs, fn, wo)


def _ffn_kernel(x_ref, xp_ref, xn_ref, mod_ref, gn_ref, wup_ref, cw_ref, cb_ref, wdn_ref, o_ref,
                xs_ref, act_ref, os_ref, *, seq):
    t = pl.program_id(0)
    tm = x_ref.shape[0]
    ext = tm + 2 * SUBLANES
    seg = ext // SUBLANES
    nslab = D_MODEL // LANES
    m = mod_ref[0, 0]
    sh2, sc2, gt2 = m[3:4], m[4:5], m[5:6]
    for c in range(nslab):
        ls = slice(c * LANES, (c + 1) * LANES)
        xs_ref[c, 0:SUBLANES, :] = xp_ref[:, ls]
        xs_ref[c, SUBLANES:SUBLANES + tm, :] = x_ref[:, ls]
        xs_ref[c, SUBLANES + tm:ext, :] = xn_ref[:, ls]
    xq = jnp.concatenate(
        [jnp.concatenate([xs_ref[c, pl.ds(i, SUBLANES, stride=seg), :] for i in range(seg)], axis=0)
         for c in range(nslab)], axis=1)
    he = _rms(xq, gn_ref[0, 2:3]) * (1.0 + sc2) + sh2
    p = lax.broadcasted_iota(jnp.int32, (ext, D_MODEL), 0)
    tile_row = (p & (SUBLANES - 1)) * seg + (p >> 3)
    lo = jnp.where(lax.rem(t * tm, seq) == 0, SUBLANES, 0)
    hi = jnp.where(lax.rem((t + 1) * tm, seq) == 0, tm + SUBLANES, ext)
    hb = jnp.where((tile_row >= lo) & (tile_row < hi), he, 0.0).astype(BF16)
    starts = [SUBLANES + r for r in range(seq, tm, seq)]
    assert all(r % seg == 0 for r in starts)
    sub = lax.broadcasted_iota(jnp.int32, (SUBLANES, FFN_CHUNK), 0)

    def conv3(up, col0):
        w = cw_ref[0, :, col0:col0 + FFN_CHUNK]
        wrap_prev = pltpu.roll(up[ext - SUBLANES:ext], 1, 0)
        wrap_next = pltpu.roll(up[0:SUBLANES], SUBLANES - 1, 0)
        for r in starts:
            wrap_prev = jnp.where(sub == r // seg, 0.0, wrap_prev)
            wrap_next = jnp.where(sub == r // seg - 1, 0.0, wrap_next)
        prev = jnp.concatenate([wrap_prev, up[0:ext - SUBLANES]], axis=0)
        nxt = jnp.concatenate([up[SUBLANES:ext], wrap_next], axis=0)
        return w[0:1] * prev + w[1:2] * up + w[2:3] * nxt + cb_ref[0, :, col0:col0 + FFN_CHUNK]

    for c in range(D_FF // FFN_CHUNK):
        ca = c * FFN_CHUNK
        cg = D_FF + c * FFN_CHUNK
        a = conv3(_dot(hb, wup_ref[0, :, ca:ca + FFN_CHUNK]), ca)
        g = conv3(_dot(hb, wup_ref[0, :, cg:cg + FFN_CHUNK]), cg)
        act_ref[:, ca:ca + FFN_CHUNK] = (a * _sigmoid(a) * g).astype(BF16)
    y = xq + gt2 * _rms(_dot(act_ref[...], wdn_ref[0]), gn_ref[0, 3:4])
    for c in range(nslab):
        for i in range(seg):
            os_ref[c, pl.ds(i, SUBLANES, stride=seg), :] = (
                y[i * SUBLANES:(i + 1) * SUBLANES, c * LANES:(c + 1) * LANES])
    for c in range(nslab):
        o_ref[:, c * LANES:(c + 1) * LANES] = os_ref[c, SUBLANES:SUBLANES + tm, :]


def _ffn(x, mod, gn, w_up, cw, cb, w_dn, *, seq, layer, latent):
    n = x.shape[0]
    tm = ROW_TILE
    nt = n // tm
    per8 = tm // SUBLANES
    last8 = n // SUBLANES - 1
    ext = tm + 2 * SUBLANES
    slab = pltpu.VMEM((D_MODEL // LANES, ext, LANES), F32)
    return pl.pallas_call(
        functools.partial(_ffn_kernel, seq=seq),
        grid=(nt,),
        in_specs=[pl.BlockSpec((tm, D_MODEL), lambda t: (t, 0)),
                  pl.BlockSpec((SUBLANES, D_MODEL), lambda t: (jnp.maximum(t * per8 - 1, 0), 0)),
                  pl.BlockSpec((SUBLANES, D_MODEL),
                               lambda t: (jnp.minimum((t + 1) * per8, last8), 0)),
                  _mod_spec(layer, 1, n // seq, nt) if latent else _mod_spec(layer, 0, 1, nt),
                  _layer_spec((4, D_MODEL), layer),
                  _layer_spec((D_MODEL, 2 * D_FF), layer),
                  _layer_spec((SUBLANES, 2 * D_FF), layer), _layer_spec((1, 2 * D_FF), layer),
                  _layer_spec((D_FF, D_MODEL), layer)],
        out_specs=pl.BlockSpec((tm, D_MODEL), lambda t: (t, 0)),
        out_shape=jax.ShapeDtypeStruct(x.shape, F32),
        scratch_shapes=[slab, pltpu.VMEM((ext, D_FF), BF16), slab],
        compiler_params=_cparams(1),
        name="ffn",
    )(x, x, x, mod, gn, w_up, cw, cb, w_dn)


def _np_consts(ctx_seq, lat_seq):
    hd = np.arange(ATTN_W) // HEAD_DIM
    pmat = (hd[:, None] == hd[None, :]).astype(np.float32) / HEAD_DIM
    ch = np.arange(FNET_W)
    same = (ch[:, None] // FNET_GROUP_W) == (ch[None, :] // FNET_GROUP_W)
    ang = 2.0 * np.pi * ((ch[:, None] % FNET_GROUP_W) * (ch[None, :] % FNET_GROUP_W)) / FNET_GROUP_W
    cc = np.where(same, np.cos(ang), 0.0).astype(np.float32)
    cs = np.where(same, np.sin(ang), 0.0).astype(np.float32)
    n = np.arange(ctx_seq)
    ang = 2.0 * np.pi * ((n[:, None] * n[None, :]) % ctx_seq) / ctx_seq
    c_ctx = np.cos(ang).astype(np.float32)
    s_ctx = np.sin(ang).astype(np.float32)
    r = np.arange(FFT_RADIX)
    ang = 2.0 * np.pi * ((r[:, None] * r[None, :]) % FFT_RADIX) / FFT_RADIX
    c64, s64 = np.cos(ang), np.sin(ang)
    eye = np.eye(FFT_BLK)
    kc, ks = np.kron(c64, eye), np.kron(s64, eye)
    k1 = np.block([[kc, ks], [-ks, kc]]).astype(np.float32)
    n_seq = FFT_RADIX * FFT_RADIX
    ang = 2.0 * np.pi * (r[:, None] * r[None, :]) / n_seq
    tc = np.repeat(np.cos(ang)[:, :, None], LANES, axis=2).astype(np.float32)
    ts = np.repeat(np.sin(ang)[:, :, None], LANES, axis=2).astype(np.float32)
    scale = 1.0 / np.sqrt(n_seq * FNET_GROUP_W)
    k2c = np.einsum("db,ec->decb", c64, eye).reshape(FFT_RADIX * FFT_BLK, FFT_BLK * FFT_RADIX)
    k2s = np.einsum("db,ec->decb", s64, eye).reshape(FFT_RADIX * FFT_BLK, FFT_BLK * FFT_RADIX)
    k2 = (np.concatenate([k2c, k2s], axis=1) * scale).astype(np.float32)
    f32 = np.float32
    rows = lat_seq // GRID_W
    row = np.repeat(np.arange(rows, dtype=f32), GRID_W)
    col = np.tile(np.arange(GRID_W, dtype=f32), rows)
    n_f = HEAD_DIM // 4
    inv = (f32(ROPE_THETA) ** (-np.arange(n_f, dtype=f32) / f32(n_f))).astype(f32)
    ang = np.concatenate([row[:, None] * inv, col[:, None] * inv], axis=-1).astype(f32)
    sign = np.tile(np.array([-1.0, 1.0], f32), HEAD_DIM // 2)
    rope_cos = np.tile(np.repeat(np.cos(ang), 2, axis=-1), (1, N_KV_HEADS)).astype(f32)
    rope_sin = np.tile(np.repeat(np.sin(ang), 2, axis=-1) * sign, (1, N_KV_HEADS)).astype(f32)
    return (pmat, cc, cs, c_ctx, s_ctx, k1, k2), (tc, ts, rope_cos, rope_sin)


def kernel(x_prompt, x_sample, cache_k, cache_v, c, c_ctx, w_ada, b_ada, g_norm, w_in, g_q, g_k, conv_w, conv_b, conv_ln_g, conv_ln_b, sgu_g, w_s, b_s, w_out, w_up, ffn_conv_w, ffn_conv_b, w_down):
    bsz, seq, _ = x_prompt.shape
    dec_b, dec_s, _ = x_sample.shape
    past = cache_k.shape[2]
    assert dec_s == FFT_RADIX * FFT_RADIX and ROW_TILE % seq == 0 and dec_s % ROW_TILE == 0
    assert 1 + dec_b <= SUBLANES

    bf_consts, f32_consts = _np_consts(seq, dec_s)
    pmat, cc, cs, c_ctx_m, s_ctx_m, k1, k2 = (jnp.asarray(a).astype(BF16) for a in bf_consts)
    tc, ts, rope_cos, rope_sin = (jnp.asarray(a) for a in f32_consts)
    rope = (rope_cos, rope_sin)

    cond8 = jnp.zeros((SUBLANES, D_MODEL), F32).at[0].set(c_ctx).at[1:1 + dec_b].set(c)
    mod = _ada(cond8, w_ada, b_ada).reshape(DEPTH, SUBLANES, 6, D_MODEL)

    vec = lambda a: a.reshape(DEPTH, 1, -1)
    gq = vec(jnp.tile(g_q, (1, N_Q_HEADS)))
    gk = vec(jnp.tile(g_k, (1, N_KV_HEADS)))
    w_up_b = w_up.astype(BF16)
    w_dn_b = w_down.astype(BF16)
    ws_b = w_s.astype(BF16)
    cw_pad = jnp.pad(conv_w, ((0, 0), (0, CONV_TAP_ROWS - CONV_K), (0, 0)))
    fcw_pad = jnp.pad(ffn_conv_w, ((0, 0), (0, SUBLANES - FFN_CONV_K), (0, 0)))
    bs_full = jnp.repeat(jnp.swapaxes(b_s, 1, 2), SGU_HEAD_W, axis=2)
    ck = cache_k.reshape(dec_b, DEPTH, past, KV_W).astype(BF16)
    cvt = jnp.transpose(cache_v, (0, 1, 3, 4, 2))
    ones_row = jnp.zeros((dec_b, DEPTH, N_KV_HEADS, VT_ROWS - HEAD_DIM, past), F32).at[:, :, :, 0].set(1.0)
    cv = jnp.concatenate([cvt, ones_row], axis=3).astype(BF16)
    front_params = (mod, g_norm, w_in, gq, gk, pmat, vec(sgu_g), cc, cs)
    conv_params = (cw_pad, vec(conv_b), vec(conv_ln_g), vec(conv_ln_b))
    ffn_params = (mod, g_norm, w_up_b, fcw_pad, vec(ffn_conv_b), w_dn_b)

    xp = x_prompt.reshape(bsz * seq, D_MODEL)
    xs = x_sample.reshape(dec_b * dec_s, D_MODEL)
    ks, vs = [], []
    for l in range(DEPTH):
        q, k, v, ci, u, vn, gr, gi = _front(xp, *front_params, None, latent=False, seq=seq, layer=l)
        ks.append(k)
        vs.append(v)
        xp = _back_ctx(xp, mod, g_norm, q, k, v, ci, *conv_params, u, vn, ws_b, bs_full,
                       gr, gi, c_ctx_m, s_ctx_m, w_out, seq=seq, layer=l)
        xp = _ffn(xp, *ffn_params, seq=seq, layer=l, latent=False)

        q, k, v, ci, u, vn, gr, gi = _front(xs, *front_params, rope, latent=True, seq=dec_s, layer=l)
        grid4 = (dec_b, FFT_RADIX, FFT_RADIX, FNET_W)
        fn = _fft(gr.reshape(grid4), gi.reshape(grid4), k1, tc, ts, k2).reshape(-1, FNET_W)
        xs = _back_lat(xs, mod, g_norm, q, k, v, ck, cv, ci, *conv_params, u, vn, ws_b, bs_full,
                       fn, w_out, seq=dec_s, layer=l)
        xs = _ffn(xs, *ffn_params, seq=dec_s, layer=l, latent=True)

    kv_shape = (bsz, DEPTH, seq, N_KV_HEADS, HEAD_DIM)
    new_k = jnp.stack([a.reshape(bsz, seq, KV_W) for a in ks], axis=1).reshape(kv_shape)
    new_v = jnp.stack([a.reshape(bsz, seq, KV_W) for a in vs], axis=1).reshape(kv_shape)
    return (xp.reshape(x_prompt.shape), xs.reshape(x_sample.shape), new_k, new_v)
```

```python
import functools

import numpy as np
import jax
import jax.numpy as jnp
from jax import lax
from jax.experimental import pallas as pl
from jax.experimental.pallas import tpu as pltpu

D_MODEL = 1024
DEPTH = 2
GRID_W = 64
HEAD_DIM = 64
N_Q_HEADS = 4
N_KV_HEADS = 2
Q_PER_KV = N_Q_HEADS // N_KV_HEADS
ATTN_W = N_Q_HEADS * HEAD_DIM
KV_W = N_KV_HEADS * HEAD_DIM
ATTN_SCALE = HEAD_DIM ** -0.5
Q_SCALE = ATTN_SCALE * float(np.log2(np.e))
ROPE_THETA = 10000.0
CONV_W = 256
CONV_K = 31
CONV_PAD = 16
CONV_TAP_ROWS = 32
SGU_W = 256
SGU_HEADS = 4
SGU_HEAD_W = SGU_W // SGU_HEADS
CHUNK = 128
FNET_W = 256
FNET_GROUPS = 4
FNET_GROUP_W = FNET_W // FNET_GROUPS
D_FF = 2816
FFN_CONV_K = 3
EPS = 1e-6
SPLIT_Q = ATTN_W
SPLIT_K = SPLIT_Q + KV_W
SPLIT_V = SPLIT_K + KV_W
SPLIT_CONV = SPLIT_V + 2 * CONV_W
SPLIT_SGU = SPLIT_CONV + 2 * SGU_W
IN_COLS = SPLIT_SGU + FNET_W
Q_HEAD_ORDER = (0, 2, 1, 3)

SUBLANES = 8
LANES = 128
FRONT_TILE = 1024
ROW_TILE = 512
LAT_TQ = 512
VT_ROWS = 80
KEY_CHUNK = 256
ATTN_LOOKAHEAD = 3
FFN_CHUNK = 256
FFT_RADIX = 64
FFT_BLK = 8
VMEM_LIMIT = 56 * 1024 * 1024

F32 = jnp.float32
BF16 = jnp.bfloat16


def _cparams(n_axes):
    return pltpu.CompilerParams(dimension_semantics=("arbitrary",) * n_axes,
                                vmem_limit_bytes=VMEM_LIMIT)


def _const_spec(shape):
    nd = len(shape)
    return pl.BlockSpec(shape, lambda *_: (0,) * nd, pipeline_mode=pl.Buffered(1))


def _layer_spec(shape, layer):
    nd = len(shape)
    return pl.BlockSpec((1,) + tuple(shape), lambda *_: (layer,) + (0,) * nd,
                        pipeline_mode=pl.Buffered(1))


def _mod_spec(layer, row0, nrows, nt):
    return pl.BlockSpec((1, 1, 6, D_MODEL), lambda t: (layer, row0 + (t * nrows) // nt, 0, 0))


def _dot(a, b):
    return jnp.dot(a, b, preferred_element_type=F32)


def _dot_nt(a, b):
    return lax.dot_general(a, b, (((1,), (1,)), ((), ())), preferred_element_type=F32)


def _sigmoid(x):
    return 1.0 / (1.0 + jnp.exp(-x))


def _rms(x, g):
    ms = jnp.mean(x * x, axis=-1, keepdims=True)
    return x * lax.rsqrt(ms + EPS) * g


def _head_rms(z, pmat, g):
    sq = z * z
    hi = sq.astype(BF16)
    lo = (sq - hi.astype(F32)).astype(BF16)
    ms = _dot(hi, pmat) + _dot(lo, pmat)
    return z * lax.rsqrt(ms + EPS) * g


def _rope(x, cos, sin_signed):
    w = x.shape[1]
    nxt = pltpu.roll(x, w - 1, 1)
    prv = pltpu.roll(x, 1, 1)
    lane = lax.broadcasted_iota(jnp.int32, x.shape, 1)
    swapped = jnp.where((lane & 1) == 0, nxt, prv)
    return x * cos + swapped * sin_signed


def _ada_kernel(cond_ref, w_ref, b_ref, o_ref):
    c = cond_ref[...]
    s = (c * _sigmoid(c)).astype(BF16)
    o_ref[0] = _dot(s, w_ref[0].astype(BF16)) + b_ref[0]


def _ada(cond8, w_ada, b_ada):
    tn = 1536
    n_out = 6 * D_MODEL
    return pl.pallas_call(
        _ada_kernel,
        grid=(DEPTH, n_out // tn),
        in_specs=[pl.BlockSpec((SUBLANES, D_MODEL), lambda l, j: (0, 0)),
                  pl.BlockSpec((1, D_MODEL, tn), lambda l, j: (l, 0, j)),
                  pl.BlockSpec((1, 1, tn), lambda l, j: (l, 0, j))],
        out_specs=pl.BlockSpec((1, SUBLANES, tn), lambda l, j: (l, 0, j)),
        out_shape=jax.ShapeDtypeStruct((DEPTH, SUBLANES, n_out), F32),
        compiler_params=_cparams(2),
        name="ada",
    )(cond8, w_ada, b_ada.reshape(DEPTH, 1, n_out))


def _front_kernel(*refs, latent):
    (x_ref, mod_ref, gn_ref, win_ref, gq_ref, gk_ref, pm_ref, sg_ref, cc_ref, cs_ref) = refs[:10]
    refs = refs[10:]
    if latent:
        cos_ref, sin_ref = refs[:2]
        refs = refs[2:]
    q_ref, k_ref, v_ref, ci_ref, u_ref, vn_ref, gr_ref, gi_ref, wb_ref = refs

    @pl.when(pl.program_id(0) == 0)
    def _():
        wq = win_ref[0, :, :SPLIT_Q]
        wb_ref[:, :SPLIT_Q] = jnp.concatenate(
            [wq[:, h * HEAD_DIM:(h + 1) * HEAD_DIM] for h in Q_HEAD_ORDER], axis=1).astype(BF16)
        wb_ref[:, SPLIT_Q:] = win_ref[0, :, SPLIT_Q:].astype(BF16)

    m = mod_ref[0, 0]
    sh1, sc1 = m[0:1], m[1:2]
    h = _rms(x_ref[...], gn_ref[0, 0:1]) * (1.0 + sc1) + sh1
    z = _dot(h.astype(BF16), wb_ref[...])
    pm = pm_ref[...]
    q = _head_rms(z[:, :SPLIT_Q], pm, gq_ref[0])
    k = _head_rms(z[:, SPLIT_Q:SPLIT_K], pm[:KV_W, :KV_W], gk_ref[0])
    v = z[:, SPLIT_K:SPLIT_V]
    if latent:
        cos = cos_ref[...]
        sin = sin_ref[...]
        q = _rope(q, jnp.concatenate([cos, cos], axis=1), jnp.concatenate([sin, sin], axis=1))
        k = _rope(k, cos, sin)
        for g, ext in enumerate(_vt_ext(v)):
            v_ref[0, g] = ext
    else:
        v_ref[...] = v
    k_ref[...] = k.astype(k_ref.dtype)
    q_ref[...] = (q * Q_SCALE).astype(q_ref.dtype)
    a = z[:, SPLIT_V:SPLIT_V + CONV_W]
    gt = z[:, SPLIT_V + CONV_W:SPLIT_CONV]
    ci_ref[...] = a * _sigmoid(gt)
    u_ref[...] = z[:, SPLIT_CONV:SPLIT_CONV + SGU_W]
    vn_ref[...] = _rms(z[:, SPLIT_CONV + SGU_W:SPLIT_SGU], sg_ref[0]).astype(vn_ref.dtype)
    ff = z[:, SPLIT_SGU:].astype(BF16)
    gr_ref[...] = _dot(ff, cc_ref[...]).astype(gr_ref.dtype)
    gi_ref[...] = (-_dot(ff, cs_ref[...])).astype(gi_ref.dtype)


def _front(x, mod, gn, w_in, gq, gk, pm, sgu_g, cc, cs, rope, *, latent, seq, layer):
    n = x.shape[0]
    tm = FRONT_TILE
    nt = n // tm
    nb = n // seq
    row = lambda w: pl.BlockSpec((tm, w), lambda t: (t, 0))
    in_specs = [row(D_MODEL),
                _mod_spec(layer, 1, nb, nt) if latent else _mod_spec(layer, 0, 1, nt),
                _layer_spec((4, D_MODEL), layer),
                _layer_spec((D_MODEL, IN_COLS), layer),
                _layer_spec((1, ATTN_W), layer), _layer_spec((1, KV_W), layer),
                _const_spec((ATTN_W, ATTN_W)), _layer_spec((1, SGU_W), layer),
                _const_spec((FNET_W, FNET_W)), _const_spec((FNET_W, FNET_W))]
    args = [x, mod, gn, w_in, gq, gk, pm, sgu_g, cc, cs]
    if latent:
        per_seq = seq // tm
        in_specs += [pl.BlockSpec((tm, KV_W), lambda t: (t % per_seq, 0))] * 2
        args += list(rope)
        kv_specs = [row(KV_W), pl.BlockSpec((1, N_KV_HEADS, VT_ROWS, tm),
                                            lambda t: (t // per_seq, 0, 0, t % per_seq))]
        kv_shapes = [jax.ShapeDtypeStruct((n, KV_W), BF16),
                     jax.ShapeDtypeStruct((nb, N_KV_HEADS, VT_ROWS, seq), BF16)]
        f_dt = F32
    else:
        kv_specs = [row(KV_W), row(KV_W)]
        kv_shapes = [jax.ShapeDtypeStruct((n, KV_W), F32)] * 2
        f_dt = BF16
    outs = [(ATTN_W, BF16), None, None, (CONV_W, F32), (SGU_W, F32), (SGU_W, BF16),
            (FNET_W, f_dt), (FNET_W, f_dt)]
    out_specs = [row(o[0]) if o else None for o in outs]
    out_shape = [jax.ShapeDtypeStruct((n, o[0]), o[1]) if o else None for o in outs]
    out_specs[1:3] = kv_specs
    out_shape[1:3] = kv_shapes
    return pl.pallas_call(
        functools.partial(_front_kernel, latent=latent),
        grid=(nt,),
        in_specs=in_specs,
        out_specs=out_specs,
        out_shape=out_shape,
        scratch_shapes=[pltpu.VMEM((D_MODEL, IN_COLS), BF16)],
        compiler_params=_cparams(1),
        name="front_lat" if latent else "front_ctx",
    )(*args)


def _fft_kernel(gr_ref, gi_ref, k1_ref, tc_ref, ts_ref, k2_ref, o_ref, yr_ref, yi_ref):
    j = pl.program_id(0)
    nb = gr_ref.shape[0]
    nblk = FFT_RADIX // FFT_BLK
    rows = FFT_RADIX * FFT_BLK
    lanes = lambda b: slice(b * FNET_W, (b + 1) * FNET_W)

    @pl.when(j < nblk)
    def _():
        g = jnp.concatenate(
            [jnp.concatenate([gr_ref[b].reshape(rows, FNET_W), gi_ref[b].reshape(rows, FNET_W)],
                             axis=0) for b in range(nb)], axis=1).astype(BF16)
        y = _dot(k1_ref[...], g)
        yr, yi = y[:rows], y[rows:]
        reps = nb * FNET_W // LANES
        tc = jnp.concatenate([tc_ref[...].reshape(rows, LANES)] * reps, axis=1)
        ts = jnp.concatenate([ts_ref[...].reshape(rows, LANES)] * reps, axis=1)
        col = pl.ds(pl.multiple_of(j * FFT_BLK, FFT_BLK), FFT_BLK)
        zr = yr * tc + yi * ts
        zi = yi * tc - yr * ts
        for b in range(nb):
            yr_ref[b, :, col, :] = zr[:, lanes(b)].reshape(FFT_RADIX, FFT_BLK, FNET_W)
            yi_ref[b, :, col, :] = zi[:, lanes(b)].reshape(FFT_RADIX, FFT_BLK, FNET_W)

    @pl.when(j >= nblk)
    def _():
        blk = pl.ds(pl.multiple_of((j - nblk) * FFT_BLK, FFT_BLK), FFT_BLK)
        y = jnp.concatenate(
            [jnp.concatenate([yr_ref[b, blk].reshape(rows, FNET_W),
                              yi_ref[b, blk].reshape(rows, FNET_W)], axis=0)
             for b in range(nb)], axis=1).astype(BF16)
        out = _dot(k2_ref[...], y)
        for b in range(nb):
            o_ref[b] = out[:, lanes(b)].reshape(FFT_RADIX, FFT_BLK, FNET_W)


def _fft(gr, gi, k1, tc, ts, k2):
    nb = gr.shape[0]
    nblk = FFT_RADIX // FFT_BLK
    rows = FFT_RADIX * FFT_BLK
    g_spec = pl.BlockSpec((nb, FFT_RADIX, FFT_BLK, FNET_W),
                          lambda j: (0, 0, jnp.minimum(j, nblk - 1), 0))
    t_spec = pl.BlockSpec((FFT_RADIX, FFT_BLK, LANES), lambda j: (0, jnp.minimum(j, nblk - 1), 0))
    return pl.pallas_call(
        _fft_kernel,
        grid=(2 * nblk,),
        in_specs=[g_spec, g_spec, _const_spec((2 * rows, 2 * rows)), t_spec, t_spec,
                  _const_spec((rows, 2 * rows))],
        out_specs=pl.BlockSpec((nb, FFT_RADIX, FFT_BLK, FNET_W),
                               lambda j: (0, 0, jnp.maximum(j - nblk, 0), 0)),
        out_shape=jax.ShapeDtypeStruct(gr.shape, F32),
        scratch_shapes=[pltpu.VMEM((nb, FFT_RADIX, FFT_RADIX, FNET_W), F32)] * 2,
        compiler_params=_cparams(1),
        name="fft",
    )(gr, gi, k1, tc, ts, k2)


def _vt_ext(v):
    vt = v.T
    n = v.shape[0]
    pad = jnp.where(lax.broadcasted_iota(jnp.int32, (VT_ROWS - HEAD_DIM, n), 0) == 0, 1.0, 0.0)
    return [jnp.concatenate([vt[g * HEAD_DIM:(g + 1) * HEAD_DIM], pad], axis=0).astype(BF16)
            for g in range(N_KV_HEADS)]


def _attend(q, k_chunks, vt_chunks):
    lane = lax.broadcasted_iota(jnp.int32, (q.shape[0], KV_W), 1)
    qm = {}
    for half in range(Q_PER_KV):
        q2 = q[:, half * KV_W:(half + 1) * KV_W]
        for g in range(N_KV_HEADS):
            in_head = (lane >= g * HEAD_DIM) & (lane < (g + 1) * HEAD_DIM)
            qm[half, g] = jnp.where(in_head, q2, jnp.zeros_like(q2))
    items = [(c, hg) for c in range(len(k_chunks)) for hg in qm]
    state = {}

    def consume(c, hg, s):
        vt = vt_chunks[c][hg[1]]
        cm = jnp.max(s, axis=0, keepdims=True)
        if hg not in state:
            state[hg] = (cm, _dot(vt, jnp.exp2(s - cm).astype(BF16)))
        else:
            m, acc = state[hg]
            m_new = jnp.maximum(m, cm)
            state[hg] = (m_new, acc * jnp.exp2(m - m_new)
                         + _dot(vt, jnp.exp2(s - m_new).astype(BF16)))

    pending = []
    for c, hg in items:
        pending.append((c, hg, _dot_nt(k_chunks[c], qm[hg])))
        if len(pending) > ATTN_LOOKAHEAD:
            consume(*pending.pop(0))
    for item in pending:
        consume(*item)
    outs = {}
    for (half, g), (_, acc) in state.items():
        outs[Q_HEAD_ORDER[Q_PER_KV * half + g]] = acc[:HEAD_DIM] / acc[HEAD_DIM:HEAD_DIM + 1]
    return jnp.concatenate([outs[h] for h in range(N_Q_HEADS)], axis=0).T


def _conv_module(win, rows, cw_ref, cb_ref, lg_ref, lb_ref):
    win_rows = rows + 2 * CONV_PAD
    shifted = [win] + [pltpu.roll(win, win_rows - sft, 0) for sft in range(1, SUBLANES)]
    cw = cw_ref[0]
    conv = jnp.zeros((rows, CONV_W), F32) + cb_ref[0]
    for j in range(CONV_K):
        off = j + CONV_PAD - CONV_K // 2
        base = (off // SUBLANES) * SUBLANES
        conv = conv + cw[j:j + 1] * shifted[off % SUBLANES][base:base + rows]
    mu = jnp.mean(conv, axis=-1, keepdims=True)
    cen = conv - mu
    var = jnp.mean(cen * cen, axis=-1, keepdims=True)
    conv = cen * lax.rsqrt(var + EPS) * lg_ref[0] + lb_ref[0]
    return conv * _sigmoid(conv)


def _spatial_gate(u, vn_ref, ws_ref, bs_ref, rows):
    lane = lax.broadcasted_iota(jnp.int32, (CHUNK, SGU_W), 1)
    mixed_chunks = []
    for c in range(rows // CHUNK):
        vn_c = vn_ref[c * CHUNK:(c + 1) * CHUNK, :]
        mixed = jnp.zeros((CHUNK, SGU_W), F32)
        for hd in range(SGU_HEADS):
            full = _dot(ws_ref[0, hd], vn_c)
            in_head = (lane >= hd * SGU_HEAD_W) & (lane < (hd + 1) * SGU_HEAD_W)
            mixed = jnp.where(in_head, full, mixed)
        mixed_chunks.append(mixed + bs_ref[0])
    return u * jnp.concatenate(mixed_chunks, axis=0)


def _load_wo(wo_ref, wob_ref):
    @pl.when(pl.program_id(0) == 0)
    def _():
        wob_ref[...] = wo_ref[0].astype(BF16)


def _mix_out(x, attn, conv, sgu, fnet, wob_ref, mod_ref, gn_ref):
    mix = (_dot(attn.astype(BF16), wob_ref[0:ATTN_W, :])
           + _dot(conv.astype(BF16), wob_ref[ATTN_W:ATTN_W + CONV_W, :])
           + _dot(sgu.astype(BF16), wob_ref[ATTN_W + CONV_W:ATTN_W + CONV_W + SGU_W, :])
           + _dot(fnet.astype(BF16), wob_ref[ATTN_W + CONV_W + SGU_W:, :]))
    gt1 = mod_ref[0, 0][2:3]
    return x + gt1 * _rms(mix, gn_ref[0, 1:2])


def _back_ctx_kernel(x_ref, mod_ref, gn_ref, q_ref, k_ref, v_ref, ci_ref, cw_ref, cb_ref,
                     lg_ref, lb_ref, u_ref, vn_ref, ws_ref, bs_ref, gr_ref, gi_ref,
                     c_ref, s_ref, wo_ref, o_ref, wob_ref, *, seq):
    _load_wo(wo_ref, wob_ref)
    rows = x_ref.shape[0]
    halo = jnp.zeros((CONV_PAD, CONV_W), F32)
    attn, conv, fnet = [], [], []
    for i in range(rows // seq):
        rs = slice(i * seq, (i + 1) * seq)
        attn.append(_attend(q_ref[rs, :], [k_ref[rs, :].astype(BF16)], [_vt_ext(v_ref[rs, :])]))
        win = jnp.concatenate([halo, ci_ref[rs, :], halo], axis=0)
        conv.append(_conv_module(win, seq, cw_ref, cb_ref, lg_ref, lb_ref))
        fnet.append((_dot(c_ref[...], gr_ref[rs, :]) + _dot(s_ref[...], gi_ref[rs, :]))
                    * (1.0 / np.sqrt(seq * FNET_GROUP_W)))
    cat = lambda parts: jnp.concatenate(parts, axis=0)
    sgu = _spatial_gate(u_ref[...], vn_ref, ws_ref, bs_ref, rows)
    o_ref[...] = _mix_out(x_ref[...], cat(attn), cat(conv), sgu, cat(fnet), wob_ref, mod_ref, gn_ref)


def _back_lat_kernel(x_ref, mod_ref, gn_ref, q_ref, k_ref, v_ref, ck_ref, cv_ref,
                     ci_ref, cip_ref, cin_ref, cw_ref, cb_ref, lg_ref, lb_ref,
                     u_ref, vn_ref, ws_ref, bs_ref, fn_ref, wo_ref, o_ref, wob_ref, *, per_seq):
    _load_wo(wo_ref, wob_ref)
    rows = x_ref.shape[0]
    t = pl.program_id(0)
    chunks = [slice(c, c + KEY_CHUNK) for c in range(0, k_ref.shape[0], KEY_CHUNK)]
    attn = _attend(q_ref[...], [k_ref[c, :] for c in chunks] + [ck_ref[0, 0]],
                   [[v_ref[0, g, :, c] for g in range(N_KV_HEADS)] for c in chunks]
                   + [[cv_ref[0, 0, g] for g in range(N_KV_HEADS)]])
    has_prev = (lax.rem(t, per_seq) > 0).astype(F32)
    has_next = (lax.rem(t, per_seq) < per_seq - 1).astype(F32)
    win = jnp.concatenate([cip_ref[...] * has_prev, ci_ref[...], cin_ref[...] * has_next], axis=0)
    conv = _conv_module(win, rows, cw_ref, cb_ref, lg_ref, lb_ref)
    sgu = _spatial_gate(u_ref[...], vn_ref, ws_ref, bs_ref, rows)
    o_ref[...] = _mix_out(x_ref[...], attn, conv, sgu, fn_ref[...], wob_ref, mod_ref, gn_ref)


def _mixer_param_specs(layer):
    vec = _layer_spec((1, CONV_W), layer)
    return ([_layer_spec((CONV_TAP_ROWS, CONV_W), layer), vec, vec, vec],
            [_layer_spec((SGU_HEADS, CHUNK, CHUNK), layer), _layer_spec((CHUNK, SGU_W), layer)])


def _back_ctx(x, mod, gn, q, k, v, ci, cw, cb, lg, lb, u, vn, ws, bs, gr, gi, cm, sm, wo,
              *, seq, layer):
    n = x.shape[0]
    tq = ROW_TILE
    row = lambda w: pl.BlockSpec((tq, w), lambda t: (t, 0))
    conv_specs, sgu_specs = _mixer_param_specs(layer)
    in_specs = ([row(D_MODEL), _mod_spec(layer, 0, 1, n // tq), _layer_spec((4, D_MODEL), layer),
                 row(ATTN_W), row(KV_W), row(KV_W), row(CONV_W)] + conv_specs
                + [row(SGU_W), row(SGU_W)] + sgu_specs
                + [row(FNET_W), row(FNET_W), _const_spec((seq, seq)), _const_spec((seq, seq)),
                   _layer_spec((D_MODEL, D_MODEL), layer)])
    return pl.pallas_call(
        functools.partial(_back_ctx_kernel, seq=seq),
        grid=(n // tq,),
        in_specs=in_specs,
        out_specs=row(D_MODEL),
        out_shape=jax.ShapeDtypeStruct(x.shape, F32),
        scratch_shapes=[pltpu.VMEM((D_MODEL, D_MODEL), BF16)],
        compiler_params=_cparams(1),
        name="back_ctx",
    )(x, mod, gn, q, k, v, ci, cw, cb, lg, lb, u, vn, ws, bs, gr, gi, cm, sm, wo)


def _back_lat(x, mod, gn, q, k, v, ck, cv, ci, cw, cb, lg, lb, u, vn, ws, bs, fn, wo,
              *, seq, layer):
    n = x.shape[0]
    tq = LAT_TQ
    nt = n // tq
    per_seq = seq // tq
    past = ck.shape[2]
    halo_per = tq // CONV_PAD
    last_halo = n // CONV_PAD - 1
    row = lambda w: pl.BlockSpec((tq, w), lambda t: (t, 0))
    conv_specs, sgu_specs = _mixer_param_specs(layer)
    in_specs = ([row(D_MODEL), _mod_spec(layer, 1, n // seq, nt), _layer_spec((4, D_MODEL), layer),
                 row(ATTN_W),
                 pl.BlockSpec((seq, KV_W), lambda t: (t // per_seq, 0)),
                 pl.BlockSpec((1, N_KV_HEADS, VT_ROWS, seq), lambda t: (t // per_seq, 0, 0, 0)),
                 pl.BlockSpec((1, 1, past, KV_W), lambda t: (t // per_seq, layer, 0, 0)),
                 pl.BlockSpec((1, 1, N_KV_HEADS, VT_ROWS, past),
                              lambda t: (t // per_seq, layer, 0, 0, 0)),
                 row(CONV_W),
                 pl.BlockSpec((CONV_PAD, CONV_W), lambda t: (jnp.maximum(t * halo_per - 1, 0), 0)),
                 pl.BlockSpec((CONV_PAD, CONV_W),
                              lambda t: (jnp.minimum((t + 1) * halo_per, last_halo), 0))]
                + conv_specs + [row(SGU_W), row(SGU_W)] + sgu_specs
                + [row(FNET_W), _layer_spec((D_MODEL, D_MODEL), layer)])
    return pl.pallas_call(
        functools.partial(_back_lat_kernel, per_seq=per_seq),
        grid=(nt,),
        in_specs=in_specs,
        out_specs=row(D_MODEL),
        out_shape=jax.ShapeDtypeStruct(x.shape, F32),
        scratch_shapes=[pltpu.VMEM((D_MODEL, D_MODEL), BF16)],
        compiler_params=_cparams(1),
        name="back_lat",
    )(x, mod, gn, q, k, v, ck, cv, ci, ci, ci, cw, cb, lg, lb, u, vn, ws, bs, fn, wo)


def _ffn_kernel(x_ref, xp_ref, xn_ref, mod_ref, gn_ref, wup_ref, cw_ref, cb_ref, wdn_ref, o_ref,
                xs_ref, act_ref, os_ref, *, seq):
    t = pl.program_id(0)
    tm = x_ref.shape[0]
    ext = tm + 2 * SUBLANES
    seg = ext // SUBLANES
    nslab = D_MODEL // LANES
    m = mod_ref[0, 0]
    sh2, sc2, gt2 = m[3:4], m[4:5], m[5:6]
    for c in range(nslab):
        ls = slice(c * LANES, (c + 1) * LANES)
        xs_ref[c, 0:SUBLANES, :] = xp_ref[:, ls]
        xs_ref[c, SUBLANES:SUBLANES + tm, :] = x_ref[:, ls]
        xs_ref[c, SUBLANES + tm:ext, :] = xn_ref[:, ls]
    xq = jnp.concatenate(
        [jnp.concatenate([xs_ref[c, pl.ds(i, SUBLANES, stride=seg), :] for i in range(seg)], axis=0)
         for c in range(nslab)], axis=1)
    he = _rms(xq, gn_ref[0, 2:3]) * (1.0 + sc2) + sh2
    p = lax.broadcasted_iota(jnp.int32, (ext, D_MODEL), 0)
    tile_row = (p & (SUBLANES - 1)) * seg + (p >> 3)
    lo = jnp.where(lax.rem(t * tm, seq) == 0, SUBLANES, 0)
    hi = jnp.where(lax.rem((t + 1) * tm, seq) == 0, tm + SUBLANES, ext)
    hb = jnp.where((tile_row >= lo) & (tile_row < hi), he, 0.0).astype(BF16)
    starts = [SUBLANES + r for r in range(seq, tm, seq)]
    assert all(r % seg == 0 for r in starts)
    sub = lax.broadcasted_iota(jnp.int32, (SUBLANES, FFN_CHUNK), 0)

    def conv3(up, col0):
        w = cw_ref[0, :, col0:col0 + FFN_CHUNK]
        wrap_prev = pltpu.roll(up[ext - SUBLANES:ext], 1, 0)
        wrap_next = pltpu.roll(up[0:SUBLANES], SUBLANES - 1, 0)
        for r in starts:
            wrap_prev = jnp.where(sub == r // seg, 0.0, wrap_prev)
            wrap_next = jnp.where(sub == r // seg - 1, 0.0, wrap_next)
        prev = jnp.concatenate([wrap_prev, up[0:ext - SUBLANES]], axis=0)
        nxt = jnp.concatenate([up[SUBLANES:ext], wrap_next], axis=0)
        return w[0:1] * prev + w[1:2] * up + w[2:3] * nxt + cb_ref[0, :, col0:col0 + FFN_CHUNK]

    for c in range(D_FF // FFN_CHUNK):
        ca = c * FFN_CHUNK
        cg = D_FF + c * FFN_CHUNK
        a = conv3(_dot(hb, wup_ref[0, :, ca:ca + FFN_CHUNK]), ca)
        g = conv3(_dot(hb, wup_ref[0, :, cg:cg + FFN_CHUNK]), cg)
        act_ref[:, ca:ca + FFN_CHUNK] = (a * _sigmoid(a) * g).astype(BF16)
    y = xq + gt2 * _rms(_dot(act_ref[...], wdn_ref[0]), gn_ref[0, 3:4])
    for c in range(nslab):
        for i in range(seg):
            os_ref[c, pl.ds(i, SUBLANES, stride=seg), :] = (
                y[i * SUBLANES:(i + 1) * SUBLANES, c * LANES:(c + 1) * LANES])
    for c in range(nslab):
        o_ref[:, c * LANES:(c + 1) * LANES] = os_ref[c, SUBLANES:SUBLANES + tm, :]


def _ffn(x, mod, gn, w_up, cw, cb, w_dn, *, seq, layer, latent):
    n = x.shape[0]
    tm = ROW_TILE
    nt = n // tm
    per8 = tm // SUBLANES
    last8 = n // SUBLANES - 1
    ext = tm + 2 * SUBLANES
    slab = pltpu.VMEM((D_MODEL // LANES, ext, LANES), F32)
    return pl.pallas_call(
        functools.partial(_ffn_kernel, seq=seq),
        grid=(nt,),
        in_specs=[pl.BlockSpec((tm, D_MODEL), lambda t: (t, 0)),
                  pl.BlockSpec((SUBLANES, D_MODEL), lambda t: (jnp.maximum(t * per8 - 1, 0), 0)),
                  pl.BlockSpec((SUBLANES, D_MODEL),
                               lambda t: (jnp.minimum((t + 1) * per8, last8), 0)),
                  _mod_spec(layer, 1, n // seq, nt) if latent else _mod_spec(layer, 0, 1, nt),
                  _layer_spec((4, D_MODEL), layer),
                  _layer_spec((D_MODEL, 2 * D_FF), layer),
                  _layer_spec((SUBLANES, 2 * D_FF), layer), _layer_spec((1, 2 * D_FF), layer),
                  _layer_spec((D_FF, D_MODEL), layer)],
        out_specs=pl.BlockSpec((tm, D_MODEL), lambda t: (t, 0)),
        out_shape=jax.ShapeDtypeStruct(x.shape, F32),
        scratch_shapes=[slab, pltpu.VMEM((ext, D_FF), BF16), slab],
        compiler_params=_cparams(1),
        name="ffn",
    )(x, x, x, mod, gn, w_up, cw, cb, w_dn)


def _np_consts(ctx_seq, lat_seq):
    hd = np.arange(ATTN_W) // HEAD_DIM
    pmat = (hd[:, None] == hd[None, :]).astype(np.float32) / HEAD_DIM
    ch = np.arange(FNET_W)
    same = (ch[:, None] // FNET_GROUP_W) == (ch[None, :] // FNET_GROUP_W)
    ang = 2.0 * np.pi * ((ch[:, None] % FNET_GROUP_W) * (ch[None, :] % FNET_GROUP_W)) / FNET_GROUP_W
    cc = np.where(same, np.cos(ang), 0.0).astype(np.float32)
    cs = np.where(same, np.sin(ang), 0.0).astype(np.float32)
    n = np.arange(ctx_seq)
    ang = 2.0 * np.pi * ((n[:, None] * n[None, :]) % ctx_seq) / ctx_seq
    c_ctx = np.cos(ang).astype(np.float32)
    s_ctx = np.sin(ang).astype(np.float32)
    r = np.arange(FFT_RADIX)
    ang = 2.0 * np.pi * ((r[:, None] * r[None, :]) % FFT_RADIX) / FFT_RADIX
    c64, s64 = np.cos(ang), np.sin(ang)
    eye = np.eye(FFT_BLK)
    kc, ks = np.kron(c64, eye), np.kron(s64, eye)
    k1 = np.block([[kc, ks], [-ks, kc]]).astype(np.float32)
    n_seq = FFT_RADIX * FFT_RADIX
    ang = 2.0 * np.pi * (r[:, None] * r[None, :]) / n_seq
    tc = np.repeat(np.cos(ang)[:, :, None], LANES, axis=2).astype(np.float32)
    ts = np.repeat(np.sin(ang)[:, :, None], LANES, axis=2).astype(np.float32)
    scale = 1.0 / np.sqrt(n_seq * FNET_GROUP_W)
    k2c = np.einsum("db,ec->decb", c64, eye).reshape(FFT_RADIX * FFT_BLK, FFT_BLK * FFT_RADIX)
    k2s = np.einsum("db,ec->decb", s64, eye).reshape(FFT_RADIX * FFT_BLK, FFT_BLK * FFT_RADIX)
    k2 = (np.concatenate([k2c, k2s], axis=1) * scale).astype(np.float32)
    f32 = np.float32
    rows = lat_seq // GRID_W
    row = np.repeat(np.arange(rows, dtype=f32), GRID_W)
    col = np.tile(np.arange(GRID_W, dtype=f32), rows)
    n_f = HEAD_DIM // 4
    inv = (f32(ROPE_THETA) ** (-np.arange(n_f, dtype=f32) / f32(n_f))).astype(f32)
    ang = np.concatenate([row[:, None] * inv, col[:, None] * inv], axis=-1).astype(f32)
    sign = np.tile(np.array([-1.0, 1.0], f32), HEAD_DIM // 2)
    rope_cos = np.tile(np.repeat(np.cos(ang), 2, axis=-1), (1, N_KV_HEADS)).astype(f32)
    rope_sin = np.tile(np.repeat(np.sin(ang), 2, axis=-1) * sign, (1, N_KV_HEADS)).astype(f32)
    return (pmat, cc, cs, c_ctx, s_ctx, k1, k2), (tc, ts, rope_cos, rope_sin)


def kernel(x_prompt, x_sample, cache_k, cache_v, c, c_ctx, w_ada, b_ada, g_norm, w_in, g_q, g_k, conv_w, conv_b, conv_ln_g, conv_ln_b, sgu_g, w_s, b_s, w_out, w_up, ffn_conv_w, ffn_conv_b, w_down):
    bsz, seq, _ = x_prompt.shape
    dec_b, dec_s, _ = x_sample.shape
    past = cache_k.shape[2]
    assert dec_s == FFT_RADIX * FFT_RADIX and ROW_TILE % seq == 0 and dec_s % ROW_TILE == 0
    assert 1 + dec_b <= SUBLANES

    bf_consts, f32_consts = _np_consts(seq, dec_s)
    pmat, cc, cs, c_ctx_m, s_ctx_m, k1, k2 = (jnp.asarray(a).astype(BF16) for a in bf_consts)
    tc, ts, rope_cos, rope_sin = (jnp.asarray(a) for a in f32_consts)
    rope = (rope_cos, rope_sin)

    cond8 = jnp.zeros((SUBLANES, D_MODEL), F32).at[0].set(c_ctx).at[1:1 + dec_b].set(c)
    mod = _ada(cond8, w_ada, b_ada).reshape(DEPTH, SUBLANES, 6, D_MODEL)

    vec = lambda a: a.reshape(DEPTH, 1, -1)
    gq = vec(jnp.tile(g_q, (1, N_Q_HEADS)))
    gk = vec(jnp.tile(g_k, (1, N_KV_HEADS)))
    w_up_b = w_up.astype(BF16)
    w_dn_b = w_down.astype(BF16)
    ws_b = w_s.astype(BF16)
    cw_pad = jnp.pad(conv_w, ((0, 0), (0, CONV_TAP_ROWS - CONV_K), (0, 0)))
    fcw_pad = jnp.pad(ffn_conv_w, ((0, 0), (0, SUBLANES - FFN_CONV_K), (0, 0)))
    bs_full = jnp.repeat(jnp.swapaxes(b_s, 1, 2), SGU_HEAD_W, axis=2)
    ck = cache_k.reshape(dec_b, DEPTH, past, KV_W).astype(BF16)
    cvt = jnp.transpose(cache_v, (0, 1, 3, 4, 2))
    ones_row = jnp.zeros((dec_b, DEPTH, N_KV_HEADS, VT_ROWS - HEAD_DIM, past), F32).at[:, :, :, 0].set(1.0)
    cv = jnp.concatenate([cvt, ones_row], axis=3).astype(BF16)
    front_params = (mod, g_norm, w_in, gq, gk, pmat, vec(sgu_g), cc, cs)
    conv_params = (cw_pad, vec(conv_b), vec(conv_ln_g), vec(conv_ln_b))
    ffn_params = (mod, g_norm, w_up_b, fcw_pad, vec(ffn_conv_b), w_dn_b)

    xp = x_prompt.reshape(bsz * seq, D_MODEL)
    xs = x_sample.reshape(dec_b * dec_s, D_MODEL)
    ks, vs = [], []
    for l in range(DEPTH):
        q, k, v, ci, u, vn, gr, gi = _front(xp, *front_params, None, latent=False, seq=seq, layer=l)
        ks.append(k)
        vs.append(v)
        xp = _back_ctx(xp, mod, g_norm, q, k, v, ci, *conv_params, u, vn, ws_b, bs_full,
                       gr, gi, c_ctx_m, s_ctx_m, w_out, seq=seq, layer=l)
        xp = _ffn(xp, *ffn_params, seq=seq, layer=l, latent=False)

        q, k, v, ci, u, vn, gr, gi = _front(xs, *front_params, rope, latent=True, seq=dec_s, layer=l)
        grid4 = (dec_b, FFT_RADIX, FFT_RADIX, FNET_W)
        fn = _fft(gr.reshape(grid4), gi.reshape(grid4), k1, tc, ts, k2).reshape(-1, FNET_W)
        xs = _back_lat(xs, mod, g_norm, q, k, v, ck, cv, ci, *conv_params, u, vn, ws_b, bs_full,
                       fn, w_out, seq=dec_s, layer=l)
        xs = _ffn(xs, *ffn_params, seq=dec_s, layer=l, latent=True)

    kv_shape = (bsz, DEPTH, seq, N_KV_HEADS, HEAD_DIM)
    new_k = jnp.stack([a.reshape(bsz, seq, KV_W) for a in ks], axis=1).reshape(kv_shape)
    new_v = jnp.stack([a.reshape(bsz, seq, KV_W) for a in vs], axis=1).reshape(kv_shape)
    return (xp.reshape(x_prompt.shape), xs.reshape(x_sample.shape), new_k, new_v)
```

```python
import functools

import numpy as np
import jax
import jax.numpy as jnp
from jax import lax
from jax.experimental import pallas as pl
from jax.experimental.pallas import tpu as pltpu

D_MODEL = 1024
DEPTH = 2
GRID_W = 64
HEAD_DIM = 64
N_Q_HEADS = 4
N_KV_HEADS = 2
Q_PER_KV = N_Q_HEADS // N_KV_HEADS
ATTN_W = N_Q_HEADS * HEAD_DIM
KV_W = N_KV_HEADS * HEAD_DIM
ATTN_SCALE = HEAD_DIM ** -0.5
Q_SCALE = ATTN_SCALE * float(np.log2(np.e))
ROPE_THETA = 10000.0
CONV_W = 256
CONV_K = 31
CONV_PAD = 16
CONV_TAP_ROWS = 32
SGU_W = 256
SGU_HEADS = 4
SGU_HEAD_W = SGU_W // SGU_HEADS
CHUNK = 128
FNET_W = 256
FNET_GROUPS = 4
FNET_GROUP_W = FNET_W // FNET_GROUPS
D_FF = 2816
FFN_CONV_K = 3
EPS = 1e-6
SPLIT_Q = ATTN_W
SPLIT_K = SPLIT_Q + KV_W
SPLIT_V = SPLIT_K + KV_W
SPLIT_CONV = SPLIT_V + 2 * CONV_W
SPLIT_SGU = SPLIT_CONV + 2 * SGU_W
IN_COLS = SPLIT_SGU + FNET_W
Q_HEAD_ORDER = (0, 2, 1, 3)

SUBLANES = 8
LANES = 128
FRONT_TILE = 1024
ROW_TILE = 512
LAT_TQ = 256
VT_ROWS = 80
KEY_CHUNK = 512
ATTN_LOOKAHEAD = 3
FFN_CHUNK = 256
FFT_RADIX = 64
FFT_BLK = 8
VMEM_LIMIT = 56 * 1024 * 1024

F32 = jnp.float32
BF16 = jnp.bfloat16


def _cparams(n_axes):
    return pltpu.CompilerParams(dimension_semantics=("arbitrary",) * n_axes,
                                vmem_limit_bytes=VMEM_LIMIT)


def _const_spec(shape):
    nd = len(shape)
    return pl.BlockSpec(shape, lambda *_: (0,) * nd, pipeline_mode=pl.Buffered(1))


def _layer_spec(shape, layer):
    nd = len(shape)
    return pl.BlockSpec((1,) + tuple(shape), lambda *_: (layer,) + (0,) * nd,
                        pipeline_mode=pl.Buffered(1))


def _mod_spec(layer, row0, nrows, nt):
    return pl.BlockSpec((1, 1, 6, D_MODEL), lambda t: (layer, row0 + (t * nrows) // nt, 0, 0))


def _dot(a, b):
    return jnp.dot(a, b, preferred_element_type=F32)


def _dot_nt(a, b):
    return lax.dot_general(a, b, (((1,), (1,)), ((), ())), preferred_element_type=F32)


def _sigmoid(x):
    return 1.0 / (1.0 + jnp.exp(-x))


def _rms(x, g):
    ms = jnp.mean(x * x, axis=-1, keepdims=True)
    return x * lax.rsqrt(ms + EPS) * g


def _head_rms(z, pmat, g):
    ms = _dot((z * z).astype(BF16), pmat)
    return z * lax.rsqrt(ms + EPS) * g


def _rope(x, cos, sin_signed):
    w = x.shape[1]
    nxt = pltpu.roll(x, w - 1, 1)
    prv = pltpu.roll(x, 1, 1)
    lane = lax.broadcasted_iota(jnp.int32, x.shape, 1)
    swapped = jnp.where((lane & 1) == 0, nxt, prv)
    return x * cos + swapped * sin_signed


def _ada_kernel(cond_ref, w_ref, b_ref, o_ref):
    c = cond_ref[...]
    s = (c * _sigmoid(c)).astype(BF16)
    o_ref[0] = _dot(s, w_ref[0].astype(BF16)) + b_ref[0]


def _ada(cond8, w_ada, b_ada):
    tn = 1536
    n_out = 6 * D_MODEL
    return pl.pallas_call(
        _ada_kernel,
        grid=(DEPTH, n_out // tn),
        in_specs=[pl.BlockSpec((SUBLANES, D_MODEL), lambda l, j: (0, 0)),
                  pl.BlockSpec((1, D_MODEL, tn), lambda l, j: (l, 0, j)),
                  pl.BlockSpec((1, 1, tn), lambda l, j: (l, 0, j))],
        out_specs=pl.BlockSpec((1, SUBLANES, tn), lambda l, j: (l, 0, j)),
        out_shape=jax.ShapeDtypeStruct((DEPTH, SUBLANES, n_out), F32),
        compiler_params=_cparams(2),
        name="ada",
    )(cond8, w_ada, b_ada.reshape(DEPTH, 1, n_out))


def _front_kernel(*refs, latent):
    (x_ref, mod_ref, gn_ref, win_ref, gq_ref, gk_ref, pm_ref, sg_ref, cc_ref, cs_ref) = refs[:10]
    refs = refs[10:]
    if latent:
        cos_ref, sin_ref = refs[:2]
        refs = refs[2:]
    q_ref, k_ref, v_ref, ci_ref, u_ref, vn_ref, gr_ref, gi_ref, wb_ref = refs

    @pl.when(pl.program_id(0) == 0)
    def _():
        wq = win_ref[0, :, :SPLIT_Q]
        wb_ref[:, :SPLIT_Q] = jnp.concatenate(
            [wq[:, h * HEAD_DIM:(h + 1) * HEAD_DIM] for h in Q_HEAD_ORDER], axis=1).astype(BF16)
        wb_ref[:, SPLIT_Q:] = win_ref[0, :, SPLIT_Q:].astype(BF16)

    m = mod_ref[0, 0]
    sh1, sc1 = m[0:1], m[1:2]
    h = _rms(x_ref[...], gn_ref[0, 0:1]) * (1.0 + sc1) + sh1
    z = _dot(h.astype(BF16), wb_ref[...])
    pm = pm_ref[...]
    q = _head_rms(z[:, :SPLIT_Q], pm, gq_ref[0])
    k = _head_rms(z[:, SPLIT_Q:SPLIT_K], pm[:KV_W, :KV_W], gk_ref[0])
    v = z[:, SPLIT_K:SPLIT_V]
    if latent:
        cos = cos_ref[...]
        sin = sin_ref[...]
        q = _rope(q, jnp.concatenate([cos, cos], axis=1), jnp.concatenate([sin, sin], axis=1))
        k = _rope(k, cos, sin)
        for g, ext in enumerate(_vt_ext(v)):
            v_ref[0, g] = ext
        k_ref[...] = k.astype(k_ref.dtype)
    else:
        k_ref[...] = k.reshape(k_ref.shape)
        v_ref[...] = v.reshape(v_ref.shape)
    q_ref[...] = (q * Q_SCALE).astype(q_ref.dtype)
    a = z[:, SPLIT_V:SPLIT_V + CONV_W]
    gt = z[:, SPLIT_V + CONV_W:SPLIT_CONV]
    ci_ref[...] = a * _sigmoid(gt)
    u_ref[...] = z[:, SPLIT_CONV:SPLIT_CONV + SGU_W]
    vn_ref[...] = _rms(z[:, SPLIT_CONV + SGU_W:SPLIT_SGU], sg_ref[0]).astype(vn_ref.dtype)
    ff = z[:, SPLIT_SGU:].astype(BF16)
    gr_ref[...] = _dot(ff, cc_ref[...]).astype(gr_ref.dtype)
    gi_ref[...] = (-_dot(ff, cs_ref[...])).astype(gi_ref.dtype)


def _front(x, mod, gn, w_in, gq, gk, pm, sgu_g, cc, cs, rope, *, latent, seq, layer):
    n = x.shape[0]
    tm = FRONT_TILE
    nt = n // tm
    nb = n // seq
    row = lambda w: pl.BlockSpec((tm, w), lambda t: (t, 0))
    in_specs = [row(D_MODEL),
                _mod_spec(layer, 1, nb, nt) if latent else _mod_spec(layer, 0, 1, nt),
                _layer_spec((4, D_MODEL), layer),
                _layer_spec((D_MODEL, IN_COLS), layer),
                _layer_spec((1, ATTN_W), layer), _layer_spec((1, KV_W), layer),
                _const_spec((ATTN_W, ATTN_W)), _layer_spec((1, SGU_W), layer),
                _const_spec((FNET_W, FNET_W)), _const_spec((FNET_W, FNET_W))]
    args = [x, mod, gn, w_in, gq, gk, pm, sgu_g, cc, cs]
    if latent:
        per_seq = seq // tm
        in_specs += [pl.BlockSpec((tm, KV_W), lambda t: (t % per_seq, 0))] * 2
        args += list(rope)
        kv_specs = [row(KV_W), pl.BlockSpec((1, N_KV_HEADS, VT_ROWS, tm),
                                            lambda t: (t // per_seq, 0, 0, t % per_seq))]
        kv_shapes = [jax.ShapeDtypeStruct((n, KV_W), BF16),
                     jax.ShapeDtypeStruct((nb, N_KV_HEADS, VT_ROWS, seq), BF16)]
        f_dt = F32
    else:
        kv_specs = [pl.BlockSpec((tm // seq, seq, KV_W), lambda t: (t, 0, 0))] * 2
        kv_shapes = [jax.ShapeDtypeStruct((nb, seq, KV_W), F32)] * 2
        f_dt = BF16
    outs = [(ATTN_W, BF16), None, None, (CONV_W, F32), (SGU_W, F32), (SGU_W, BF16),
            (FNET_W, f_dt), (FNET_W, f_dt)]
    out_specs = [row(o[0]) if o else None for o in outs]
    out_shape = [jax.ShapeDtypeStruct((n, o[0]), o[1]) if o else None for o in outs]
    out_specs[1:3] = kv_specs
    out_shape[1:3] = kv_shapes
    return pl.pallas_call(
        functools.partial(_front_kernel, latent=latent),
        grid=(nt,),
        in_specs=in_specs,
        out_specs=out_specs,
        out_shape=out_shape,
        scratch_shapes=[pltpu.VMEM((D_MODEL, IN_COLS), BF16)],
        compiler_params=_cparams(1),
        name="front_lat" if latent else "front_ctx",
    )(*args)


def _fft_kernel(gr_ref, gi_ref, k1_ref, tc_ref, ts_ref, k2_ref, o_ref, yr_ref, yi_ref):
    j = pl.program_id(0)
    nb = gr_ref.shape[0]
    nblk = FFT_RADIX // FFT_BLK
    rows = FFT_RADIX * FFT_BLK
    lanes = lambda b: slice(b * FNET_W, (b + 1) * FNET_W)

    @pl.when(j < nblk)
    def _():
        g = jnp.concatenate(
            [jnp.concatenate([gr_ref[b].reshape(rows, FNET_W), gi_ref[b].reshape(rows, FNET_W)],
                             axis=0) for b in range(nb)], axis=1).astype(BF16)
        y = _dot(k1_ref[...], g)
        yr, yi = y[:rows], y[rows:]
        reps = nb * FNET_W // LANES
        tc = jnp.concatenate([tc_ref[...].reshape(rows, LANES)] * reps, axis=1)
        ts = jnp.concatenate([ts_ref[...].reshape(rows, LANES)] * reps, axis=1)
        col = pl.ds(pl.multiple_of(j * FFT_BLK, FFT_BLK), FFT_BLK)
        zr = yr * tc + yi * ts
        zi = yi * tc - yr * ts
        for b in range(nb):
            yr_ref[b, :, col, :] = zr[:, lanes(b)].reshape(FFT_RADIX, FFT_BLK, FNET_W)
            yi_ref[b, :, col, :] = zi[:, lanes(b)].reshape(FFT_RADIX, FFT_BLK, FNET_W)

    @pl.when(j >= nblk)
    def _():
        blk = pl.ds(pl.multiple_of((j - nblk) * FFT_BLK, FFT_BLK), FFT_BLK)
        y = jnp.concatenate(
            [jnp.concatenate([yr_ref[b, blk].reshape(rows, FNET_W),
                              yi_ref[b, blk].reshape(rows, FNET_W)], axis=0)
             for b in range(nb)], axis=1).astype(BF16)
        out = _dot(k2_ref[...], y)
        for b in range(nb):
            o_ref[b] = out[:, lanes(b)].reshape(FFT_RADIX, FFT_BLK, FNET_W)


def _fft(gr, gi, k1, tc, ts, k2):
    nb = gr.shape[0]
    nblk = FFT_RADIX // FFT_BLK
    rows = FFT_RADIX * FFT_BLK
    g_spec = pl.BlockSpec((nb, FFT_RADIX, FFT_BLK, FNET_W),
                          lambda j: (0, 0, jnp.minimum(j, nblk - 1), 0))
    t_spec = pl.BlockSpec((FFT_RADIX, FFT_BLK, LANES), lambda j: (0, jnp.minimum(j, nblk - 1), 0))
    return pl.pallas_call(
        _fft_kernel,
        grid=(2 * nblk,),
        in_specs=[g_spec, g_spec, _const_spec((2 * rows, 2 * rows)), t_spec, t_spec,
                  _const_spec((rows, 2 * rows))],
        out_specs=pl.BlockSpec((nb, FFT_RADIX, FFT_BLK, FNET_W),
                               lambda j: (0, 0, jnp.maximum(j - nblk, 0), 0)),
        out_shape=jax.ShapeDtypeStruct(gr.shape, F32),
        scratch_shapes=[pltpu.VMEM((nb, FFT_RADIX, FFT_RADIX, FNET_W), F32)] * 2,
        compiler_params=_cparams(1),
        name="fft",
    )(gr, gi, k1, tc, ts, k2)


def _vt_ext(v):
    vt = v.T
    n = v.shape[0]
    pad = jnp.where(lax.broadcasted_iota(jnp.int32, (VT_ROWS - HEAD_DIM, n), 0) == 0, 1.0, 0.0)
    return [jnp.concatenate([vt[g * HEAD_DIM:(g + 1) * HEAD_DIM], pad], axis=0).astype(BF16)
            for g in range(N_KV_HEADS)]


def _attend(q, k_chunks, vt_chunks):
    lane = lax.broadcasted_iota(jnp.int32, (q.shape[0], KV_W), 1)
    qm = {}
    for half in range(Q_PER_KV):
        q2 = q[:, half * KV_W:(half + 1) * KV_W]
        for g in range(N_KV_HEADS):
            in_head = (lane >= g * HEAD_DIM) & (lane < (g + 1) * HEAD_DIM)
            qm[half, g] = jnp.where(in_head, q2, jnp.zeros_like(q2))
    items = [(c, hg) for c in range(len(k_chunks)) for hg in qm]
    state = {}

    def consume(c, hg, s):
        vt = vt_chunks[c][hg[1]]
        cm = jnp.max(s, axis=0, keepdims=True)
        if hg not in state:
            state[hg] = (cm, _dot(vt, jnp.exp2(s - cm).astype(BF16)))
        else:
            m, acc = state[hg]
            m_new = jnp.maximum(m, cm)
            state[hg] = (m_new, acc * jnp.exp2(m - m_new)
                         + _dot(vt, jnp.exp2(s - m_new).astype(BF16)))

    pending = []
    for c, hg in items:
        pending.append((c, hg, _dot_nt(k_chunks[c], qm[hg])))
        if len(pending) > ATTN_LOOKAHEAD:
            consume(*pending.pop(0))
    for item in pending:
        consume(*item)
    outs = {}
    for (half, g), (_, acc) in state.items():
        outs[Q_HEAD_ORDER[Q_PER_KV * half + g]] = acc[:HEAD_DIM] / acc[HEAD_DIM:HEAD_DIM + 1]
    return jnp.concatenate([outs[h] for h in range(N_Q_HEADS)], axis=0).T


def _conv_module(win, rows, cw_ref, cb_ref, lg_ref, lb_ref):
    win_rows = rows + 2 * CONV_PAD
    shifted = [win] + [pltpu.roll(win, win_rows - sft, 0) for sft in range(1, SUBLANES)]
    cw = cw_ref[0]
    conv = jnp.zeros((rows, CONV_W), F32) + cb_ref[0]
    for j in range(CONV_K):
        off = j + CONV_PAD - CONV_K // 2
        base = (off // SUBLANES) * SUBLANES
        conv = conv + cw[j:j + 1] * shifted[off % SUBLANES][base:base + rows]
    mu = jnp.mean(conv, axis=-1, keepdims=True)
    cen = conv - mu
    var = jnp.mean(cen * cen, axis=-1, keepdims=True)
    conv = cen * lax.rsqrt(var + EPS) * lg_ref[0] + lb_ref[0]
    return conv * _sigmoid(conv)


def _spatial_gate(u, vn_ref, ws_ref, bs_ref, rows):
    lane = lax.broadcasted_iota(jnp.int32, (CHUNK, SGU_W), 1)
    mixed_chunks = []
    for c in range(rows // CHUNK):
        vn_c = vn_ref[c * CHUNK:(c + 1) * CHUNK, :]
        mixed = jnp.zeros((CHUNK, SGU_W), F32)
        for hd in range(SGU_HEADS):
            full = _dot(ws_ref[0, hd], vn_c)
            in_head = (lane >= hd * SGU_HEAD_W) & (lane < (hd + 1) * SGU_HEAD_W)
            mixed = jnp.where(in_head, full, mixed)
        mixed_chunks.append(mixed + bs_ref[0])
    return u * jnp.concatenate(mixed_chunks, axis=0)


def _load_wo(wo_ref, wob_ref):
    @pl.when(pl.program_id(0) == 0)
    def _():
        wob_ref[...] = wo_ref[0].astype(BF16)


def _mix_out(x, attn, conv, sgu, fnet, wob_ref, mod_ref, gn_ref):
    mix = (_dot(attn.astype(BF16), wob_ref[0:ATTN_W, :])
           + _dot(conv.astype(BF16), wob_ref[ATTN_W:ATTN_W + CONV_W, :])
           + _dot(sgu.astype(BF16), wob_ref[ATTN_W + CONV_W:ATTN_W + CONV_W + SGU_W, :])
           + _dot(fnet.astype(BF16), wob_ref[ATTN_W + CONV_W + SGU_W:, :]))
    gt1 = mod_ref[0, 0][2:3]
    return x + gt1 * _rms(mix, gn_ref[0, 1:2])


def _back_ctx_kernel(x_ref, mod_ref, gn_ref, q_ref, k_ref, v_ref, ci_ref, cw_ref, cb_ref,
                     lg_ref, lb_ref, u_ref, vn_ref, ws_ref, bs_ref, gr_ref, gi_ref,
                     c_ref, s_ref, wo_ref, o_ref, wob_ref, *, seq):
    _load_wo(wo_ref, wob_ref)
    rows = x_ref.shape[0]
    halo = jnp.zeros((CONV_PAD, CONV_W), F32)
    attn, conv, fnet = [], [], []
    for i in range(rows // seq):
        rs = slice(i * seq, (i + 1) * seq)
        attn.append(_attend(q_ref[rs, :], [k_ref[i].astype(BF16)], [_vt_ext(v_ref[i])]))
        win = jnp.concatenate([halo, ci_ref[rs, :], halo], axis=0)
        conv.append(_conv_module(win, seq, cw_ref, cb_ref, lg_ref, lb_ref))
        fnet.append((_dot(c_ref[...], gr_ref[rs, :]) + _dot(s_ref[...], gi_ref[rs, :]))
                    * (1.0 / np.sqrt(seq * FNET_GROUP_W)))
    cat = lambda parts: jnp.concatenate(parts, axis=0)
    sgu = _spatial_gate(u_ref[...], vn_ref, ws_ref, bs_ref, rows)
    o_ref[...] = _mix_out(x_ref[...], cat(attn), cat(conv), sgu, cat(fnet), wob_ref, mod_ref, gn_ref)


def _back_lat_kernel(x_ref, mod_ref, gn_ref, q_ref, k_ref, v_ref, ck_ref, cv_ref,
                     ci_ref, cip_ref, cin_ref, cw_ref, cb_ref, lg_ref, lb_ref,
                     u_ref, vn_ref, ws_ref, bs_ref, fn_ref, wo_ref, o_ref, wob_ref, *, per_seq):
    _load_wo(wo_ref, wob_ref)
    rows = x_ref.shape[0]
    t = pl.program_id(0)
    chunks = [slice(c, c + KEY_CHUNK) for c in range(0, k_ref.shape[0], KEY_CHUNK)]
    attn = _attend(q_ref[...], [k_ref[c, :] for c in chunks] + [ck_ref[0, 0]],
                   [[v_ref[0, g, :, c] for g in range(N_KV_HEADS)] for c in chunks]
                   + [[cv_ref[0, 0, g] for g in range(N_KV_HEADS)]])
    has_prev = (lax.rem(t, per_seq) > 0).astype(F32)
    has_next = (lax.rem(t, per_seq) < per_seq - 1).astype(F32)
    win = jnp.concatenate([cip_ref[...] * has_prev, ci_ref[...], cin_ref[...] * has_next], axis=0)
    conv = _conv_module(win, rows, cw_ref, cb_ref, lg_ref, lb_ref)
    sgu = _spatial_gate(u_ref[...], vn_ref, ws_ref, bs_ref, rows)
    o_ref[...] = _mix_out(x_ref[...], attn, conv, sgu, fn_ref[...], wob_ref, mod_ref, gn_ref)


def _mixer_param_specs(layer):
    vec = _layer_spec((1, CONV_W), layer)
    return ([_layer_spec((CONV_TAP_ROWS, CONV_W), layer), vec, vec, vec],
            [_layer_spec((SGU_HEADS, CHUNK, CHUNK), layer), _layer_spec((CHUNK, SGU_W), layer)])


def _back_ctx(x, mod, gn, q, k, v, ci, cw, cb, lg, lb, u, vn, ws, bs, gr, gi, cm, sm, wo,
              *, seq, layer):
    n = x.shape[0]
    tq = ROW_TILE
    row = lambda w: pl.BlockSpec((tq, w), lambda t: (t, 0))
    conv_specs, sgu_specs = _mixer_param_specs(layer)
    in_specs = ([row(D_MODEL), _mod_spec(layer, 0, 1, n // tq), _layer_spec((4, D_MODEL), layer),
                 row(ATTN_W)] + [pl.BlockSpec((tq // seq, seq, KV_W), lambda t: (t, 0, 0))] * 2
                + [row(CONV_W)] + conv_specs
                + [row(SGU_W), row(SGU_W)] + sgu_specs
                + [row(FNET_W), row(FNET_W), _const_spec((seq, seq)), _const_spec((seq, seq)),
                   _layer_spec((D_MODEL, D_MODEL), layer)])
    return pl.pallas_call(
        functools.partial(_back_ctx_kernel, seq=seq),
        grid=(n // tq,),
        in_specs=in_specs,
        out_specs=row(D_MODEL),
        out_shape=jax.ShapeDtypeStruct(x.shape, F32),
        scratch_shapes=[pltpu.VMEM((D_MODEL, D_MODEL), BF16)],
        compiler_params=_cparams(1),
        name="back_ctx",
    )(x, mod, gn, q, k, v, ci, cw, cb, lg, lb, u, vn, ws, bs, gr, gi, cm, sm, wo)


def _back_lat(x, mod, gn, q, k, v, ck, cv, ci, cw, cb, lg, lb, u, vn, ws, bs, fn, wo,
              *, seq, layer):
    n = x.shape[0]
    tq = LAT_TQ
    nt = n // tq
    per_seq = seq // tq
    past = ck.shape[2]
    halo_per = tq // CONV_PAD
    last_halo = n // CONV_PAD - 1
    row = lambda w: pl.BlockSpec((tq, w), lambda t: (t, 0))
    conv_specs, sgu_specs = _mixer_param_specs(layer)
    in_specs = ([row(D_MODEL), _mod_spec(layer, 1, n // seq, nt), _layer_spec((4, D_MODEL), layer),
                 row(ATTN_W),
                 pl.BlockSpec((seq, KV_W), lambda t: (t // per_seq, 0)),
                 pl.BlockSpec((1, N_KV_HEADS, VT_ROWS, seq), lambda t: (t // per_seq, 0, 0, 0)),
                 pl.BlockSpec((1, 1, past, KV_W), lambda t: (t // per_seq, layer, 0, 0)),
                 pl.BlockSpec((1, 1, N_KV_HEADS, VT_ROWS, past),
                              lambda t: (t // per_seq, layer, 0, 0, 0)),
                 row(CONV_W),
                 pl.BlockSpec((CONV_PAD, CONV_W), lambda t: (jnp.maximum(t * halo_per - 1, 0), 0)),
                 pl.BlockSpec((CONV_PAD, CONV_W),
                              lambda t: (jnp.minimum((t + 1) * halo_per, last_halo), 0))]
                + conv_specs + [row(SGU_W), row(SGU_W)] + sgu_specs
                + [row(FNET_W), _layer_spec((D_MODEL, D_MODEL), layer)])
    return pl.pallas_call(
        functools.partial(_back_lat_kernel, per_seq=per_seq),
        grid=(nt,),
        in_specs=in_specs,
        out_specs=row(D_MODEL),
        out_shape=jax.ShapeDtypeStruct(x.shape, F32),
        scratch_shapes=[pltpu.VMEM((D_MODEL, D_MODEL), BF16)],
        compiler_params=_cparams(1),
        name="back_lat",
    )(x, mod, gn, q, k, v, ck, cv, ci, ci, ci, cw, cb, lg, lb, u, vn, ws, bs, fn, wo)


def _ffn_kernel(x_ref, xp_ref, xn_ref, mod_ref, gn_ref, wup_ref, cw_ref, cb_ref, wdn_ref, o_ref,
                xs_ref, act_ref, os_ref, *, seq):
    t = pl.program_id(0)
    tm = x_ref.shape[0]
    ext = tm + 2 * SUBLANES
    seg = ext // SUBLANES
    nslab = D_MODEL // LANES
    m = mod_ref[0, 0]
    sh2, sc2, gt2 = m[3:4], m[4:5], m[5:6]
    for c in range(nslab):
        ls = slice(c * LANES, (c + 1) * LANES)
        xs_ref[c, 0:SUBLANES, :] = xp_ref[:, ls]
        xs_ref[c, SUBLANES:SUBLANES + tm, :] = x_ref[:, ls]
        xs_ref[c, SUBLANES + tm:ext, :] = xn_ref[:, ls]
    xq = jnp.concatenate(
        [jnp.concatenate([xs_ref[c, pl.ds(i, SUBLANES, stride=seg), :] for i in range(seg)], axis=0)
         for c in range(nslab)], axis=1)
    he = _rms(xq, gn_ref[0, 2:3]) * (1.0 + sc2) + sh2
    p = lax.broadcasted_iota(jnp.int32, (ext, D_MODEL), 0)
    tile_row = (p & (SUBLANES - 1)) * seg + (p >> 3)
    lo = jnp.where(lax.rem(t * tm, seq) == 0, SUBLANES, 0)
    hi = jnp.where(lax.rem((t + 1) * tm, seq) == 0, tm + SUBLANES, ext)
    hb = jnp.where((tile_row >= lo) & (tile_row < hi), he, 0.0).astype(BF16)
    starts = [SUBLANES + r for r in range(seq, tm, seq)]
    assert all(r % seg == 0 for r in starts)
    sub = lax.broadcasted_iota(jnp.int32, (SUBLANES, FFN_CHUNK), 0)

    def conv3(up, col0):
        w = cw_ref[0, :, col0:col0 + FFN_CHUNK]
        wrap_prev = pltpu.roll(up[ext - SUBLANES:ext], 1, 0)
        wrap_next = pltpu.roll(up[0:SUBLANES], SUBLANES - 1, 0)
        for r in starts:
            wrap_prev = jnp.where(sub == r // seg, 0.0, wrap_prev)
            wrap_next = jnp.where(sub == r // seg - 1, 0.0, wrap_next)
        prev = jnp.concatenate([wrap_prev, up[0:ext - SUBLANES]], axis=0)
        nxt = jnp.concatenate([up[SUBLANES:ext], wrap_next], axis=0)
        return w[0:1] * prev + w[1:2] * up + w[2:3] * nxt + cb_ref[0, :, col0:col0 + FFN_CHUNK]

    for c in range(D_FF // FFN_CHUNK):
        ca = c * FFN_CHUNK
        cg = D_FF + c * FFN_CHUNK
        a = conv3(_dot(hb, wup_ref[0, :, ca:ca + FFN_CHUNK]), ca)
        g = conv3(_dot(hb, wup_ref[0, :, cg:cg + FFN_CHUNK]), cg)
        act_ref[:, ca:ca + FFN_CHUNK] = (a * _sigmoid(a) * g).astype(BF16)
    y = xq + gt2 * _rms(_dot(act_ref[...], wdn_ref[0]), gn_ref[0, 3:4])
    for c in range(nslab):
        for i in range(seg):
            os_ref[c, pl.ds(i, SUBLANES, stride=seg), :] = (
                y[i * SUBLANES:(i + 1) * SUBLANES, c * LANES:(c + 1) * LANES])
    for c in range(nslab):
        o_ref[:, c * LANES:(c + 1) * LANES] = os_ref[c, SUBLANES:SUBLANES + tm, :]


def _ffn(x, mod, gn, w_up, cw, cb, w_dn, *, seq, layer, latent):
    n = x.shape[0]
    tm = ROW_TILE
    nt = n // tm
    per8 = tm // SUBLANES
    last8 = n // SUBLANES - 1
    ext = tm + 2 * SUBLANES
    slab = pltpu.VMEM((D_MODEL // LANES, ext, LANES), F32)
    return pl.pallas_call(
        functools.partial(_ffn_kernel, seq=seq),
        grid=(nt,),
        in_specs=[pl.BlockSpec((tm, D_MODEL), lambda t: (t, 0)),
                  pl.BlockSpec((SUBLANES, D_MODEL), lambda t: (jnp.maximum(t * per8 - 1, 0), 0)),
                  pl.BlockSpec((SUBLANES, D_MODEL),
                               lambda t: (jnp.minimum((t + 1) * per8, last8), 0)),
                  _mod_spec(layer, 1, n // seq, nt) if latent else _mod_spec(layer, 0, 1, nt),
                  _layer_spec((4, D_MODEL), layer),
                  _layer_spec((D_MODEL, 2 * D_FF), layer),
                  _layer_spec((SUBLANES, 2 * D_FF), layer), _layer_spec((1, 2 * D_FF), layer),
                  _layer_spec((D_FF, D_MODEL), layer)],
        out_specs=pl.BlockSpec((tm, D_MODEL), lambda t: (t, 0)),
        out_shape=jax.ShapeDtypeStruct(x.shape, F32),
        scratch_shapes=[slab, pltpu.VMEM((ext, D_FF), BF16), slab],
        compiler_params=_cparams(1),
        name="ffn",
    )(x, x, x, mod, gn, w_up, cw, cb, w_dn)


def _np_consts(ctx_seq, lat_seq):
    hd = np.arange(ATTN_W) // HEAD_DIM
    pmat = (hd[:, None] == hd[None, :]).astype(np.float32) / HEAD_DIM
    ch = np.arange(FNET_W)
    same = (ch[:, None] // FNET_GROUP_W) == (ch[None, :] // FNET_GROUP_W)
    ang = 2.0 * np.pi * ((ch[:, None] % FNET_GROUP_W) * (ch[None, :] % FNET_GROUP_W)) / FNET_GROUP_W
    cc = np.where(same, np.cos(ang), 0.0).astype(np.float32)
    cs = np.where(same, np.sin(ang), 0.0).astype(np.float32)
    n = np.arange(ctx_seq)
    ang = 2.0 * np.pi * ((n[:, None] * n[None, :]) % ctx_seq) / ctx_seq
    c_ctx = np.cos(ang).astype(np.float32)
    s_ctx = np.sin(ang).astype(np.float32)
    r = np.arange(FFT_RADIX)
    ang = 2.0 * np.pi * ((r[:, None] * r[None, :]) % FFT_RADIX) / FFT_RADIX
    c64, s64 = np.cos(ang), np.sin(ang)
    eye = np.eye(FFT_BLK)
    kc, ks = np.kron(c64, eye), np.kron(s64, eye)
    k1 = np.block([[kc, ks], [-ks, kc]]).astype(np.float32)
    n_seq = FFT_RADIX * FFT_RADIX
    ang = 2.0 * np.pi * (r[:, None] * r[None, :]) / n_seq
    tc = np.repeat(np.cos(ang)[:, :, None], LANES, axis=2).astype(np.float32)
    ts = np.repeat(np.sin(ang)[:, :, None], LANES, axis=2).astype(np.float32)
    scale = 1.0 / np.sqrt(n_seq * FNET_GROUP_W)
    k2c = np.einsum("db,ec->decb", c64, eye).reshape(FFT_RADIX * FFT_BLK, FFT_BLK * FFT_RADIX)
    k2s = np.einsum("db,ec->decb", s64, eye).reshape(FFT_RADIX * FFT_BLK, FFT_BLK * FFT_RADIX)
    k2 = (np.concatenate([k2c, k2s], axis=1) * scale).astype(np.float32)
    f32 = np.float32
    rows = lat_seq // GRID_W
    row = np.repeat(np.arange(rows, dtype=f32), GRID_W)
    col = np.tile(np.arange(GRID_W, dtype=f32), rows)
    n_f = HEAD_DIM // 4
    inv = (f32(ROPE_THETA) ** (-np.arange(n_f, dtype=f32) / f32(n_f))).astype(f32)
    ang = np.concatenate([row[:, None] * inv, col[:, None] * inv], axis=-1).astype(f32)
    sign = np.tile(np.array([-1.0, 1.0], f32), HEAD_DIM // 2)
    rope_cos = np.tile(np.repeat(np.cos(ang), 2, axis=-1), (1, N_KV_HEADS)).astype(f32)
    rope_sin = np.tile(np.repeat(np.sin(ang), 2, axis=-1) * sign, (1, N_KV_HEADS)).astype(f32)
    return (pmat, cc, cs, c_ctx, s_ctx, k1, k2), (tc, ts, rope_cos, rope_sin)


def kernel(x_prompt, x_sample, cache_k, cache_v, c, c_ctx, w_ada, b_ada, g_norm, w_in, g_q, g_k, conv_w, conv_b, conv_ln_g, conv_ln_b, sgu_g, w_s, b_s, w_out, w_up, ffn_conv_w, ffn_conv_b, w_down):
    bsz, seq, _ = x_prompt.shape
    dec_b, dec_s, _ = x_sample.shape
    past = cache_k.shape[2]
    assert dec_s == FFT_RADIX * FFT_RADIX and ROW_TILE % seq == 0 and dec_s % ROW_TILE == 0
    assert 1 + dec_b <= SUBLANES

    bf_consts, f32_consts = _np_consts(seq, dec_s)
    pmat, cc, cs, c_ctx_m, s_ctx_m, k1, k2 = (jnp.asarray(a).astype(BF16) for a in bf_consts)
    tc, ts, rope_cos, rope_sin = (jnp.asarray(a) for a in f32_consts)
    rope = (rope_cos, rope_sin)

    cond8 = jnp.zeros((SUBLANES, D_MODEL), F32).at[0].set(c_ctx).at[1:1 + dec_b].set(c)
    mod = _ada(cond8, w_ada, b_ada).reshape(DEPTH, SUBLANES, 6, D_MODEL)

    vec = lambda a: a.reshape(DEPTH, 1, -1)
    gq = vec(jnp.tile(g_q, (1, N_Q_HEADS)))
    gk = vec(jnp.tile(g_k, (1, N_KV_HEADS)))
    w_up_b = w_up.astype(BF16)
    w_dn_b = w_down.astype(BF16)
    ws_b = w_s.astype(BF16)
    cw_pad = jnp.pad(conv_w, ((0, 0), (0, CONV_TAP_ROWS - CONV_K), (0, 0)))
    fcw_pad = jnp.pad(ffn_conv_w, ((0, 0), (0, SUBLANES - FFN_CONV_K), (0, 0)))
    bs_full = jnp.repeat(jnp.swapaxes(b_s, 1, 2), SGU_HEAD_W, axis=2)
    ck = cache_k.reshape(dec_b, DEPTH, past, KV_W).astype(BF16)
    cvt = jnp.transpose(cache_v, (0, 1, 3, 4, 2))
    ones_row = jnp.zeros((dec_b, DEPTH, N_KV_HEADS, VT_ROWS - HEAD_DIM, past), F32).at[:, :, :, 0].set(1.0)
    cv = jnp.concatenate([cvt, ones_row], axis=3).astype(BF16)
    front_params = (mod, g_norm, w_in, gq, gk, pmat, vec(sgu_g), cc, cs)
    conv_params = (cw_pad, vec(conv_b), vec(conv_ln_g), vec(conv_ln_b))
    ffn_params = (mod, g_norm, w_up_b, fcw_pad, vec(ffn_conv_b), w_dn_b)

    xp = x_prompt.reshape(bsz * seq, D_MODEL)
    xs = x_sample.reshape(dec_b * dec_s, D_MODEL)
    ks, vs = [], []
    for l in range(DEPTH):
        q, k, v, ci, u, vn, gr, gi = _front(xp, *front_params, None, latent=False, seq=seq, layer=l)
        ks.append(k)
        vs.append(v)
        xp = _back_ctx(xp, mod, g_norm, q, k, v, ci, *conv_params, u, vn, ws_b, bs_full,
                       gr, gi, c_ctx_m, s_ctx_m, w_out, seq=seq, layer=l)
        xp = _ffn(xp, *ffn_params, seq=seq, layer=l, latent=False)

        q, k, v, ci, u, vn, gr, gi = _front(xs, *front_params, rope, latent=True, seq=dec_s, layer=l)
        grid4 = (dec_b, FFT_RADIX, FFT_RADIX, FNET_W)
        fn = _fft(gr.reshape(grid4), gi.reshape(grid4), k1, tc, ts, k2).reshape(-1, FNET_W)
        xs = _back_lat(xs, mod, g_norm, q, k, v, ck, cv, ci, *conv_params, u, vn, ws_b, bs_full,
                       fn, w_out, seq=dec_s, layer=l)
        xs = _ffn(xs, *ffn_params, seq=dec_s, layer=l, latent=True)

    kv_shape = (bsz, DEPTH, seq, N_KV_HEADS, HEAD_DIM)
    new_k = jnp.stack(ks, axis=1).reshape(kv_shape)
    new_v = jnp.stack(vs, axis=1).reshape(kv_shape)
    return (xp.reshape(x_prompt.shape), xs.reshape(x_sample.shape), new_k, new_v)
```

```python
import functools

import numpy as np
import jax
import jax.numpy as jnp
from jax import lax
from jax.experimental import pallas as pl
from jax.experimental.pallas import tpu as pltpu

D_MODEL = 1024
DEPTH = 2
GRID_W = 64
HEAD_DIM = 64
N_Q_HEADS = 4
N_KV_HEADS = 2
Q_PER_KV = N_Q_HEADS // N_KV_HEADS
ATTN_W = N_Q_HEADS * HEAD_DIM
KV_W = N_KV_HEADS * HEAD_DIM
ATTN_SCALE = HEAD_DIM ** -0.5
Q_SCALE = ATTN_SCALE * float(np.log2(np.e))
ROPE_THETA = 10000.0
CONV_W = 256
CONV_K = 31
CONV_PAD = 16
CONV_TAP_ROWS = 32
SGU_W = 256
SGU_HEADS = 4
SGU_HEAD_W = SGU_W // SGU_HEADS
CHUNK = 128
FNET_W = 256
FNET_GROUPS = 4
FNET_GROUP_W = FNET_W // FNET_GROUPS
D_FF = 2816
FFN_CONV_K = 3
EPS = 1e-6
SPLIT_Q = ATTN_W
SPLIT_K = SPLIT_Q + KV_W
SPLIT_V = SPLIT_K + KV_W
SPLIT_CONV = SPLIT_V + 2 * CONV_W
SPLIT_SGU = SPLIT_CONV + 2 * SGU_W
IN_COLS = SPLIT_SGU + FNET_W
Q_HEAD_ORDER = (0, 2, 1, 3)

SUBLANES = 8
LANES = 128
FRONT_TILE = 1024
ROW_TILE = 512
FFN_SUBTILES = 2
LAT_TQ = 256
VT_ROWS = 80
KEY_CHUNK = 512
ATTN_LOOKAHEAD = 3
FFN_CHUNK = 256
FFT_RADIX = 64
FFT_BLK = 8
VMEM_LIMIT = 56 * 1024 * 1024

F32 = jnp.float32
BF16 = jnp.bfloat16


def _cparams(n_axes):
    return pltpu.CompilerParams(dimension_semantics=("arbitrary",) * n_axes,
                                vmem_limit_bytes=VMEM_LIMIT)


def _const_spec(shape):
    nd = len(shape)
    return pl.BlockSpec(shape, lambda *_: (0,) * nd, pipeline_mode=pl.Buffered(1))


def _layer_spec(shape, layer):
    nd = len(shape)
    return pl.BlockSpec((1,) + tuple(shape), lambda *_: (layer,) + (0,) * nd,
                        pipeline_mode=pl.Buffered(1))


def _mod_spec(layer, row0, nrows, nt):
    return pl.BlockSpec((1, 1, 6, D_MODEL), lambda t: (layer, row0 + (t * nrows) // nt, 0, 0))


def _dot(a, b):
    return jnp.dot(a, b, preferred_element_type=F32)


def _dot_nt(a, b):
    return lax.dot_general(a, b, (((1,), (1,)), ((), ())), preferred_element_type=F32)


def _sigmoid(x):
    return 1.0 / (1.0 + jnp.exp(-x))


def _rms(x, g):
    ms = jnp.mean(x * x, axis=-1, keepdims=True)
    return x * lax.rsqrt(ms + EPS) * g


def _head_rms(z, pmat, g):
    ms = _dot((z * z).astype(BF16), pmat)
    return z * lax.rsqrt(ms + EPS) * g


def _rope(x, cos, sin_signed):
    w = x.shape[1]
    nxt = pltpu.roll(x, w - 1, 1)
    prv = pltpu.roll(x, 1, 1)
    lane = lax.broadcasted_iota(jnp.int32, x.shape, 1)
    swapped = jnp.where((lane & 1) == 0, nxt, prv)
    return x * cos + swapped * sin_signed


def _ada_kernel(cond_ref, w_ref, b_ref, o_ref):
    c = cond_ref[...]
    s = (c * _sigmoid(c)).astype(BF16)
    o_ref[0] = _dot(s, w_ref[0].astype(BF16)) + b_ref[0]


def _ada(cond8, w_ada, b_ada):
    tn = 1536
    n_out = 6 * D_MODEL
    return pl.pallas_call(
        _ada_kernel,
        grid=(DEPTH, n_out // tn),
        in_specs=[pl.BlockSpec((SUBLANES, D_MODEL), lambda l, j: (0, 0)),
                  pl.BlockSpec((1, D_MODEL, tn), lambda l, j: (l, 0, j)),
                  pl.BlockSpec((1, 1, tn), lambda l, j: (l, 0, j))],
        out_specs=pl.BlockSpec((1, SUBLANES, tn), lambda l, j: (l, 0, j)),
        out_shape=jax.ShapeDtypeStruct((DEPTH, SUBLANES, n_out), F32),
        compiler_params=_cparams(2),
        name="ada",
    )(cond8, w_ada, b_ada.reshape(DEPTH, 1, n_out))


def _front_kernel(*refs, latent):
    (x_ref, mod_ref, gn_ref, win_ref, gq_ref, gk_ref, pm_ref, sg_ref, cc_ref, cs_ref) = refs[:10]
    refs = refs[10:]
    if latent:
        cos_ref, sin_ref = refs[:2]
        refs = refs[2:]
    q_ref, k_ref, v_ref, ci_ref, u_ref, vn_ref, gr_ref, gi_ref, wb_ref = refs

    @pl.when(pl.program_id(0) == 0)
    def _():
        wq = win_ref[0, :, :SPLIT_Q]
        wb_ref[:, :SPLIT_Q] = jnp.concatenate(
            [wq[:, h * HEAD_DIM:(h + 1) * HEAD_DIM] for h in Q_HEAD_ORDER], axis=1).astype(BF16)
        wb_ref[:, SPLIT_Q:] = win_ref[0, :, SPLIT_Q:].astype(BF16)

    m = mod_ref[0, 0]
    sh1, sc1 = m[0:1], m[1:2]
    h = _rms(x_ref[...], gn_ref[0, 0:1]) * (1.0 + sc1) + sh1
    z = _dot(h.astype(BF16), wb_ref[...])
    pm = pm_ref[...]
    q = _head_rms(z[:, :SPLIT_Q], pm, gq_ref[0])
    k = _head_rms(z[:, SPLIT_Q:SPLIT_K], pm[:KV_W, :KV_W], gk_ref[0])
    v = z[:, SPLIT_K:SPLIT_V]
    if latent:
        cos = cos_ref[...]
        sin = sin_ref[...]
        q = _rope(q, jnp.concatenate([cos, cos], axis=1), jnp.concatenate([sin, sin], axis=1))
        k = _rope(k, cos, sin)
        for g, ext in enumerate(_vt_ext(v)):
            v_ref[0, g] = ext
        k_ref[...] = k.astype(k_ref.dtype)
    else:
        k_ref[...] = k.reshape(k_ref.shape)
        v_ref[...] = v.reshape(v_ref.shape)
    q_ref[...] = (q * Q_SCALE).astype(q_ref.dtype)
    a = z[:, SPLIT_V:SPLIT_V + CONV_W]
    gt = z[:, SPLIT_V + CONV_W:SPLIT_CONV]
    ci_ref[...] = a * _sigmoid(gt)
    u_ref[...] = z[:, SPLIT_CONV:SPLIT_CONV + SGU_W]
    vn_ref[...] = _rms(z[:, SPLIT_CONV + SGU_W:SPLIT_SGU], sg_ref[0]).astype(vn_ref.dtype)
    ff = z[:, SPLIT_SGU:].astype(BF16)
    gr_ref[...] = _dot(ff, cc_ref[...]).astype(gr_ref.dtype)
    gi_ref[...] = (-_dot(ff, cs_ref[...])).astype(gi_ref.dtype)


def _front(x, mod, gn, w_in, gq, gk, pm, sgu_g, cc, cs, rope, *, latent, seq, layer):
    n = x.shape[0]
    tm = FRONT_TILE
    nt = n // tm
    nb = n // seq
    row = lambda w: pl.BlockSpec((tm, w), lambda t: (t, 0))
    in_specs = [row(D_MODEL),
                _mod_spec(layer, 1, nb, nt) if latent else _mod_spec(layer, 0, 1, nt),
                _layer_spec((4, D_MODEL), layer),
                _layer_spec((D_MODEL, IN_COLS), layer),
                _layer_spec((1, ATTN_W), layer), _layer_spec((1, KV_W), layer),
                _const_spec((ATTN_W, ATTN_W)), _layer_spec((1, SGU_W), layer),
                _const_spec((FNET_W, FNET_W)), _const_spec((FNET_W, FNET_W))]
    args = [x, mod, gn, w_in, gq, gk, pm, sgu_g, cc, cs]
    if latent:
        per_seq = seq // tm
        in_specs += [pl.BlockSpec((tm, KV_W), lambda t: (t % per_seq, 0))] * 2
        args += list(rope)
        kv_specs = [row(KV_W), pl.BlockSpec((1, N_KV_HEADS, VT_ROWS, tm),
                                            lambda t: (t // per_seq, 0, 0, t % per_seq))]
        kv_shapes = [jax.ShapeDtypeStruct((n, KV_W), BF16),
                     jax.ShapeDtypeStruct((nb, N_KV_HEADS, VT_ROWS, seq), BF16)]
        f_dt = F32
    else:
        kv_specs = [pl.BlockSpec((tm // seq, seq, KV_W), lambda t: (t, 0, 0))] * 2
        kv_shapes = [jax.ShapeDtypeStruct((nb, seq, KV_W), F32)] * 2
        f_dt = BF16
    outs = [(ATTN_W, BF16), None, None, (CONV_W, F32), (SGU_W, F32), (SGU_W, BF16),
            (FNET_W, f_dt), (FNET_W, f_dt)]
    out_specs = [row(o[0]) if o else None for o in outs]
    out_shape = [jax.ShapeDtypeStruct((n, o[0]), o[1]) if o else None for o in outs]
    out_specs[1:3] = kv_specs
    out_shape[1:3] = kv_shapes
    return pl.pallas_call(
        functools.partial(_front_kernel, latent=latent),
        grid=(nt,),
        in_specs=in_specs,
        out_specs=out_specs,
        out_shape=out_shape,
        scratch_shapes=[pltpu.VMEM((D_MODEL, IN_COLS), BF16)],
        compiler_params=_cparams(1),
        name="front_lat" if latent else "front_ctx",
    )(*args)


def _fft_kernel(gr_ref, gi_ref, k1_ref, tc_ref, ts_ref, k2_ref, o_ref, yr_ref, yi_ref):
    j = pl.program_id(0)
    nb = gr_ref.shape[0]
    nblk = FFT_RADIX // FFT_BLK
    rows = FFT_RADIX * FFT_BLK
    lanes = lambda b: slice(b * FNET_W, (b + 1) * FNET_W)

    @pl.when(j < nblk)
    def _():
        g = jnp.concatenate(
            [jnp.concatenate([gr_ref[b].reshape(rows, FNET_W), gi_ref[b].reshape(rows, FNET_W)],
                             axis=0) for b in range(nb)], axis=1).astype(BF16)
        y = _dot(k1_ref[...], g)
        yr, yi = y[:rows], y[rows:]
        reps = nb * FNET_W // LANES
        tc = jnp.concatenate([tc_ref[...].reshape(rows, LANES)] * reps, axis=1)
        ts = jnp.concatenate([ts_ref[...].reshape(rows, LANES)] * reps, axis=1)
        col = pl.ds(pl.multiple_of(j * FFT_BLK, FFT_BLK), FFT_BLK)
        zr = yr * tc + yi * ts
        zi = yi * tc - yr * ts
        for b in range(nb):
            yr_ref[b, :, col, :] = zr[:, lanes(b)].reshape(FFT_RADIX, FFT_BLK, FNET_W)
            yi_ref[b, :, col, :] = zi[:, lanes(b)].reshape(FFT_RADIX, FFT_BLK, FNET_W)

    @pl.when(j >= nblk)
    def _():
        blk = pl.ds(pl.multiple_of((j - nblk) * FFT_BLK, FFT_BLK), FFT_BLK)
        y = jnp.concatenate(
            [jnp.concatenate([yr_ref[b, blk].reshape(rows, FNET_W),
                              yi_ref[b, blk].reshape(rows, FNET_W)], axis=0)
             for b in range(nb)], axis=1).astype(BF16)
        out = _dot(k2_ref[...], y)
        for b in range(nb):
            o_ref[b] = out[:, lanes(b)].reshape(FFT_RADIX, FFT_BLK, FNET_W)


def _fft(gr, gi, k1, tc, ts, k2):
    nb = gr.shape[0]
    nblk = FFT_RADIX // FFT_BLK
    rows = FFT_RADIX * FFT_BLK
    g_spec = pl.BlockSpec((nb, FFT_RADIX, FFT_BLK, FNET_W),
                          lambda j: (0, 0, jnp.minimum(j, nblk - 1), 0))
    t_spec = pl.BlockSpec((FFT_RADIX, FFT_BLK, LANES), lambda j: (0, jnp.minimum(j, nblk - 1), 0))
    return pl.pallas_call(
        _fft_kernel,
        grid=(2 * nblk,),
        in_specs=[g_spec, g_spec, _const_spec((2 * rows, 2 * rows)), t_spec, t_spec,
                  _const_spec((rows, 2 * rows))],
        out_specs=pl.BlockSpec((nb, FFT_RADIX, FFT_BLK, FNET_W),
                               lambda j: (0, 0, jnp.maximum(j - nblk, 0), 0)),
        out_shape=jax.ShapeDtypeStruct(gr.shape, F32),
        scratch_shapes=[pltpu.VMEM((nb, FFT_RADIX, FFT_RADIX, FNET_W), F32)] * 2,
        compiler_params=_cparams(1),
        name="fft",
    )(gr, gi, k1, tc, ts, k2)


def _vt_ext(v):
    vt = v.T
    n = v.shape[0]
    pad = jnp.where(lax.broadcasted_iota(jnp.int32, (VT_ROWS - HEAD_DIM, n), 0) == 0, 1.0, 0.0)
    return [jnp.concatenate([vt[g * HEAD_DIM:(g + 1) * HEAD_DIM], pad], axis=0).astype(BF16)
            for g in range(N_KV_HEADS)]


def _attend(q, k_chunks, vt_chunks):
    lane = lax.broadcasted_iota(jnp.int32, (q.shape[0], KV_W), 1)
    qm = {}
    for half in range(Q_PER_KV):
        q2 = q[:, half * KV_W:(half + 1) * KV_W]
        for g in range(N_KV_HEADS):
            in_head = (lane >= g * HEAD_DIM) & (lane < (g + 1) * HEAD_DIM)
            qm[half, g] = jnp.where(in_head, q2, jnp.zeros_like(q2))
    items = [(c, hg) for c in range(len(k_chunks)) for hg in qm]
    state = {}

    def consume(c, hg, s):
        vt = vt_chunks[c][hg[1]]
        cm = jnp.max(s, axis=0, keepdims=True)
        if hg not in state:
            state[hg] = (cm, _dot(vt, jnp.exp2(s - cm).astype(BF16)))
        else:
            m, acc = state[hg]
            m_new = jnp.maximum(m, cm)
            state[hg] = (m_new, acc * jnp.exp2(m - m_new)
                         + _dot(vt, jnp.exp2(s - m_new).astype(BF16)))

    pending = []
    for c, hg in items:
        pending.append((c, hg, _dot_nt(k_chunks[c], qm[hg])))
        if len(pending) > ATTN_LOOKAHEAD:
            consume(*pending.pop(0))
    for item in pending:
        consume(*item)
    outs = {}
    for (half, g), (_, acc) in state.items():
        outs[Q_HEAD_ORDER[Q_PER_KV * half + g]] = acc[:HEAD_DIM] / acc[HEAD_DIM:HEAD_DIM + 1]
    return jnp.concatenate([outs[h] for h in range(N_Q_HEADS)], axis=0).T


def _conv_module(win, rows, cw_ref, cb_ref, lg_ref, lb_ref):
    win_rows = rows + 2 * CONV_PAD
    shifted = [win] + [pltpu.roll(win, win_rows - sft, 0) for sft in range(1, SUBLANES)]
    cw = cw_ref[0]
    conv = jnp.zeros((rows, CONV_W), F32) + cb_ref[0]
    for j in range(CONV_K):
        off = j + CONV_PAD - CONV_K // 2
        base = (off // SUBLANES) * SUBLANES
        conv = conv + cw[j:j + 1] * shifted[off % SUBLANES][base:base + rows]
    mu = jnp.mean(conv, axis=-1, keepdims=True)
    cen = conv - mu
    var = jnp.mean(cen * cen, axis=-1, keepdims=True)
    conv = cen * lax.rsqrt(var + EPS) * lg_ref[0] + lb_ref[0]
    return conv * _sigmoid(conv)


def _spatial_gate(u, vn_ref, ws_ref, bs_ref, rows):
    lane = lax.broadcasted_iota(jnp.int32, (CHUNK, SGU_W), 1)
    mixed_chunks = []
    for c in range(rows // CHUNK):
        vn_c = vn_ref[c * CHUNK:(c + 1) * CHUNK, :]
        mixed = jnp.zeros((CHUNK, SGU_W), F32)
        for hd in range(SGU_HEADS):
            full = _dot(ws_ref[0, hd], vn_c)
            in_head = (lane >= hd * SGU_HEAD_W) & (lane < (hd + 1) * SGU_HEAD_W)
            mixed = jnp.where(in_head, full, mixed)
        mixed_chunks.append(mixed + bs_ref[0])
    return u * jnp.concatenate(mixed_chunks, axis=0)


def _load_wo(wo_ref, wob_ref):
    @pl.when(pl.program_id(0) == 0)
    def _():
        wob_ref[...] = wo_ref[0].astype(BF16)


def _mix_out(x, attn, conv, sgu, fnet, wob_ref, mod_ref, gn_ref):
    mix = (_dot(attn.astype(BF16), wob_ref[0:ATTN_W, :])
           + _dot(conv.astype(BF16), wob_ref[ATTN_W:ATTN_W + CONV_W, :])
           + _dot(sgu.astype(BF16), wob_ref[ATTN_W + CONV_W:ATTN_W + CONV_W + SGU_W, :])
           + _dot(fnet.astype(BF16), wob_ref[ATTN_W + CONV_W + SGU_W:, :]))
    gt1 = mod_ref[0, 0][2:3]
    return x + gt1 * _rms(mix, gn_ref[0, 1:2])


def _back_ctx_kernel(x_ref, mod_ref, gn_ref, q_ref, k_ref, v_ref, ci_ref, cw_ref, cb_ref,
                     lg_ref, lb_ref, u_ref, vn_ref, ws_ref, bs_ref, gr_ref, gi_ref,
                     c_ref, s_ref, wo_ref, o_ref, wob_ref, *, seq):
    _load_wo(wo_ref, wob_ref)
    rows = x_ref.shape[0]
    halo = jnp.zeros((CONV_PAD, CONV_W), F32)
    attn, conv, fnet = [], [], []
    for i in range(rows // seq):
        rs = slice(i * seq, (i + 1) * seq)
        attn.append(_attend(q_ref[rs, :], [k_ref[i].astype(BF16)], [_vt_ext(v_ref[i])]))
        win = jnp.concatenate([halo, ci_ref[rs, :], halo], axis=0)
        conv.append(_conv_module(win, seq, cw_ref, cb_ref, lg_ref, lb_ref))
        fnet.append((_dot(c_ref[...], gr_ref[rs, :]) + _dot(s_ref[...], gi_ref[rs, :]))
                    * (1.0 / np.sqrt(seq * FNET_GROUP_W)))
    cat = lambda parts: jnp.concatenate(parts, axis=0)
    sgu = _spatial_gate(u_ref[...], vn_ref, ws_ref, bs_ref, rows)
    o_ref[...] = _mix_out(x_ref[...], cat(attn), cat(conv), sgu, cat(fnet), wob_ref, mod_ref, gn_ref)


def _back_lat_kernel(x_ref, mod_ref, gn_ref, q_ref, k_ref, v_ref, ck_ref, cv_ref,
                     ci_ref, cip_ref, cin_ref, cw_ref, cb_ref, lg_ref, lb_ref,
                     u_ref, vn_ref, ws_ref, bs_ref, fn_ref, wo_ref, o_ref, wob_ref, *, per_seq):
    _load_wo(wo_ref, wob_ref)
    rows = x_ref.shape[0]
    t = pl.program_id(0)
    chunks = [slice(c, c + KEY_CHUNK) for c in range(0, k_ref.shape[0], KEY_CHUNK)]
    attn = _attend(q_ref[...], [k_ref[c, :] for c in chunks] + [ck_ref[0, 0]],
                   [[v_ref[0, g, :, c] for g in range(N_KV_HEADS)] for c in chunks]
                   + [[cv_ref[0, 0, g] for g in range(N_KV_HEADS)]])
    has_prev = (lax.rem(t, per_seq) > 0).astype(F32)
    has_next = (lax.rem(t, per_seq) < per_seq - 1).astype(F32)
    win = jnp.concatenate([cip_ref[...] * has_prev, ci_ref[...], cin_ref[...] * has_next], axis=0)
    conv = _conv_module(win, rows, cw_ref, cb_ref, lg_ref, lb_ref)
    sgu = _spatial_gate(u_ref[...], vn_ref, ws_ref, bs_ref, rows)
    o_ref[...] = _mix_out(x_ref[...], attn, conv, sgu, fn_ref[...], wob_ref, mod_ref, gn_ref)


def _mixer_param_specs(layer):
    vec = _layer_spec((1, CONV_W), layer)
    return ([_layer_spec((CONV_TAP_ROWS, CONV_W), layer), vec, vec, vec],
            [_layer_spec((SGU_HEADS, CHUNK, CHUNK), layer), _layer_spec((CHUNK, SGU_W), layer)])


def _back_ctx(x, mod, gn, q, k, v, ci, cw, cb, lg, lb, u, vn, ws, bs, gr, gi, cm, sm, wo,
              *, seq, layer):
    n = x.shape[0]
    tq = ROW_TILE
    row = lambda w: pl.BlockSpec((tq, w), lambda t: (t, 0))
    conv_specs, sgu_specs = _mixer_param_specs(layer)
    in_specs = ([row(D_MODEL), _mod_spec(layer, 0, 1, n // tq), _layer_spec((4, D_MODEL), layer),
                 row(ATTN_W)] + [pl.BlockSpec((tq // seq, seq, KV_W), lambda t: (t, 0, 0))] * 2
                + [row(CONV_W)] + conv_specs
                + [row(SGU_W), row(SGU_W)] + sgu_specs
                + [row(FNET_W), row(FNET_W), _const_spec((seq, seq)), _const_spec((seq, seq)),
                   _layer_spec((D_MODEL, D_MODEL), layer)])
    return pl.pallas_call(
        functools.partial(_back_ctx_kernel, seq=seq),
        grid=(n // tq,),
        in_specs=in_specs,
        out_specs=row(D_MODEL),
        out_shape=jax.ShapeDtypeStruct(x.shape, F32),
        scratch_shapes=[pltpu.VMEM((D_MODEL, D_MODEL), BF16)],
        compiler_params=_cparams(1),
        name="back_ctx",
    )(x, mod, gn, q, k, v, ci, cw, cb, lg, lb, u, vn, ws, bs, gr, gi, cm, sm, wo)


def _back_lat(x, mod, gn, q, k, v, ck, cv, ci, cw, cb, lg, lb, u, vn, ws, bs, fn, wo,
              *, seq, layer):
    n = x.shape[0]
    tq = LAT_TQ
    nt = n // tq
    per_seq = seq // tq
    past = ck.shape[2]
    halo_per = tq // CONV_PAD
    last_halo = n // CONV_PAD - 1
    row = lambda w: pl.BlockSpec((tq, w), lambda t: (t, 0))
    conv_specs, sgu_specs = _mixer_param_specs(layer)
    in_specs = ([row(D_MODEL), _mod_spec(layer, 1, n // seq, nt), _layer_spec((4, D_MODEL), layer),
                 row(ATTN_W),
                 pl.BlockSpec((seq, KV_W), lambda t: (t // per_seq, 0)),
                 pl.BlockSpec((1, N_KV_HEADS, VT_ROWS, seq), lambda t: (t // per_seq, 0, 0, 0)),
                 pl.BlockSpec((1, 1, past, KV_W), lambda t: (t // per_seq, layer, 0, 0)),
                 pl.BlockSpec((1, 1, N_KV_HEADS, VT_ROWS, past),
                              lambda t: (t // per_seq, layer, 0, 0, 0)),
                 row(CONV_W),
                 pl.BlockSpec((CONV_PAD, CONV_W), lambda t: (jnp.maximum(t * halo_per - 1, 0), 0)),
                 pl.BlockSpec((CONV_PAD, CONV_W),
                              lambda t: (jnp.minimum((t + 1) * halo_per, last_halo), 0))]
                + conv_specs + [row(SGU_W), row(SGU_W)] + sgu_specs
                + [row(FNET_W), _layer_spec((D_MODEL, D_MODEL), layer)])
    return pl.pallas_call(
        functools.partial(_back_lat_kernel, per_seq=per_seq),
        grid=(nt,),
        in_specs=in_specs,
        out_specs=row(D_MODEL),
        out_shape=jax.ShapeDtypeStruct(x.shape, F32),
        scratch_shapes=[pltpu.VMEM((D_MODEL, D_MODEL), BF16)],
        compiler_params=_cparams(1),
        name="back_lat",
    )(x, mod, gn, q, k, v, ck, cv, ci, ci, ci, cw, cb, lg, lb, u, vn, ws, bs, fn, wo)


def _ffn_kernel(x_ref, xp_ref, xn_ref, mod_ref, gn_ref, wup_ref, cw_ref, cb_ref, wdn_ref, o_ref,
                xs_ref, act_ref, os_ref, *, seq):
    t = pl.program_id(0)
    tm = x_ref.shape[0] // FFN_SUBTILES
    ext = tm + 2 * SUBLANES
    seg = ext // SUBLANES
    m = mod_ref[0, 0]
    sh2, sc2, gt2 = m[3:4], m[4:5], m[5:6]
    p = lax.broadcasted_iota(jnp.int32, (ext, D_MODEL), 0)
    tile_row = (p & (SUBLANES - 1)) * seg + (p >> 3)
    starts = [SUBLANES + r for r in range(seq, tm, seq)]
    assert all(r % seg == 0 for r in starts)
    sub = lax.broadcasted_iota(jnp.int32, (SUBLANES, FFN_CHUNK), 0)
    for h in range(FFN_SUBTILES):
        r0 = h * tm
        _ffn_subtile(
            x_ref, r0, tm,
            xp_ref[...] if h == 0 else x_ref[r0 - SUBLANES:r0, :],
            xn_ref[...] if h == FFN_SUBTILES - 1 else x_ref[r0 + tm:r0 + tm + SUBLANES, :],
            (t * FFN_SUBTILES + h) * tm, seq, (sh2, sc2, gt2), tile_row, starts, sub,
            gn_ref, wup_ref, cw_ref, cb_ref, wdn_ref, o_ref, xs_ref.at[h], act_ref.at[h],
            os_ref.at[h])


def _ffn_subtile(x_ref, r0, tm, halo_prev, halo_next, row0, seq, mods, tile_row, starts, sub,
                 gn_ref, wup_ref, cw_ref, cb_ref, wdn_ref, o_ref, xs_ref, act_ref, os_ref):
    sh2, sc2, gt2 = mods
    ext = tm + 2 * SUBLANES
    seg = ext // SUBLANES
    nslab = D_MODEL // LANES
    for c in range(nslab):
        ls = slice(c * LANES, (c + 1) * LANES)
        xs_ref[c, 0:SUBLANES, :] = halo_prev[:, ls]
        xs_ref[c, SUBLANES:SUBLANES + tm, :] = x_ref[r0:r0 + tm, ls]
        xs_ref[c, SUBLANES + tm:ext, :] = halo_next[:, ls]
    xq = jnp.concatenate(
        [jnp.concatenate([xs_ref[c, pl.ds(i, SUBLANES, stride=seg), :] for i in range(seg)], axis=0)
         for c in range(nslab)], axis=1)
    he = _rms(xq, gn_ref[0, 2:3]) * (1.0 + sc2) + sh2
    lo = jnp.where(lax.rem(row0, seq) == 0, SUBLANES, 0)
    hi = jnp.where(lax.rem(row0 + tm, seq) == 0, tm + SUBLANES, ext)
    hb = jnp.where((tile_row >= lo) & (tile_row < hi), he, 0.0).astype(BF16)

    def conv3(up, col0):
        w = cw_ref[0, :, col0:col0 + FFN_CHUNK]
        wrap_prev = pltpu.roll(up[ext - SUBLANES:ext], 1, 0)
        wrap_next = pltpu.roll(up[0:SUBLANES], SUBLANES - 1, 0)
        for r in starts:
            wrap_prev = jnp.where(sub == r // seg, 0.0, wrap_prev)
            wrap_next = jnp.where(sub == r // seg - 1, 0.0, wrap_next)
        prev = jnp.concatenate([wrap_prev, up[0:ext - SUBLANES]], axis=0)
        nxt = jnp.concatenate([up[SUBLANES:ext], wrap_next], axis=0)
        return w[0:1] * prev + w[1:2] * up + w[2:3] * nxt + cb_ref[0, :, col0:col0 + FFN_CHUNK]

    for c in range(D_FF // FFN_CHUNK):
        ca = c * FFN_CHUNK
        cg = D_FF + c * FFN_CHUNK
        a = conv3(_dot(hb, wup_ref[0, :, ca:ca + FFN_CHUNK]), ca)
        g = conv3(_dot(hb, wup_ref[0, :, cg:cg + FFN_CHUNK]), cg)
        act_ref[:, ca:ca + FFN_CHUNK] = (a * _sigmoid(a) * g).astype(BF16)
    y = xq + gt2 * _rms(_dot(act_ref[...], wdn_ref[0]), gn_ref[0, 3:4])
    for c in range(nslab):
        for i in range(seg):
            os_ref[c, pl.ds(i, SUBLANES, stride=seg), :] = (
                y[i * SUBLANES:(i + 1) * SUBLANES, c * LANES:(c + 1) * LANES])
    for c in range(nslab):
        o_ref[r0:r0 + tm, c * LANES:(c + 1) * LANES] = os_ref[c, SUBLANES:SUBLANES + tm, :]


def _ffn(x, mod, gn, w_up, cw, cb, w_dn, *, seq, layer, latent):
    n = x.shape[0]
    tm = FFN_SUBTILES * ROW_TILE
    nt = n // tm
    per8 = tm // SUBLANES
    last8 = n // SUBLANES - 1
    ext = ROW_TILE + 2 * SUBLANES
    slab = pltpu.VMEM((FFN_SUBTILES, D_MODEL // LANES, ext, LANES), F32)
    return pl.pallas_call(
        functools.partial(_ffn_kernel, seq=seq),
        grid=(nt,),
        in_specs=[pl.BlockSpec((tm, D_MODEL), lambda t: (t, 0)),
                  pl.BlockSpec((SUBLANES, D_MODEL), lambda t: (jnp.maximum(t * per8 - 1, 0), 0)),
                  pl.BlockSpec((SUBLANES, D_MODEL),
                               lambda t: (jnp.minimum((t + 1) * per8, last8), 0)),
                  _mod_spec(layer, 1, n // seq, nt) if latent else _mod_spec(layer, 0, 1, nt),
                  _layer_spec((4, D_MODEL), layer),
                  _layer_spec((D_MODEL, 2 * D_FF), layer),
                  _layer_spec((SUBLANES, 2 * D_FF), layer), _layer_spec((1, 2 * D_FF), layer),
                  _layer_spec((D_FF, D_MODEL), layer)],
        out_specs=pl.BlockSpec((tm, D_MODEL), lambda t: (t, 0)),
        out_shape=jax.ShapeDtypeStruct(x.shape, F32),
        scratch_shapes=[slab, pltpu.VMEM((FFN_SUBTILES, ext, D_FF), BF16), slab],
        compiler_params=_cparams(1),
        name="ffn",
    )(x, x, x, mod, gn, w_up, cw, cb, w_dn)


def _np_consts(ctx_seq, lat_seq):
    hd = np.arange(ATTN_W) // HEAD_DIM
    pmat = (hd[:, None] == hd[None, :]).astype(np.float32) / HEAD_DIM
    ch = np.arange(FNET_W)
    same = (ch[:, None] // FNET_GROUP_W) == (ch[None, :] // FNET_GROUP_W)
    ang = 2.0 * np.pi * ((ch[:, None] % FNET_GROUP_W) * (ch[None, :] % FNET_GROUP_W)) / FNET_GROUP_W
    cc = np.where(same, np.cos(ang), 0.0).astype(np.float32)
    cs = np.where(same, np.sin(ang), 0.0).astype(np.float32)
    n = np.arange(ctx_seq)
    ang = 2.0 * np.pi * ((n[:, None] * n[None, :]) % ctx_seq) / ctx_seq
    c_ctx = np.cos(ang).astype(np.float32)
    s_ctx = np.sin(ang).astype(np.float32)
    r = np.arange(FFT_RADIX)
    ang = 2.0 * np.pi * ((r[:, None] * r[None, :]) % FFT_RADIX) / FFT_RADIX
    c64, s64 = np.cos(ang), np.sin(ang)
    eye = np.eye(FFT_BLK)
    kc, ks = np.kron(c64, eye), np.kron(s64, eye)
    k1 = np.block([[kc, ks], [-ks, kc]]).astype(np.float32)
    n_seq = FFT_RADIX * FFT_RADIX
    ang = 2.0 * np.pi * (r[:, None] * r[None, :]) / n_seq
    tc = np.repeat(np.cos(ang)[:, :, None], LANES, axis=2).astype(np.float32)
    ts = np.repeat(np.sin(ang)[:, :, None], LANES, axis=2).astype(np.float32)
    scale = 1.0 / np.sqrt(n_seq * FNET_GROUP_W)
    k2c = np.einsum("db,ec->decb", c64, eye).reshape(FFT_RADIX * FFT_BLK, FFT_BLK * FFT_RADIX)
    k2s = np.einsum("db,ec->decb", s64, eye).reshape(FFT_RADIX * FFT_BLK, FFT_BLK * FFT_RADIX)
    k2 = (np.concatenate([k2c, k2s], axis=1) * scale).astype(np.float32)
    f32 = np.float32
    rows = lat_seq // GRID_W
    row = np.repeat(np.arange(rows, dtype=f32), GRID_W)
    col = np.tile(np.arange(GRID_W, dtype=f32), rows)
    n_f = HEAD_DIM // 4
    inv = (f32(ROPE_THETA) ** (-np.arange(n_f, dtype=f32) / f32(n_f))).astype(f32)
    ang = np.concatenate([row[:, None] * inv, col[:, None] * inv], axis=-1).astype(f32)
    sign = np.tile(np.array([-1.0, 1.0], f32), HEAD_DIM // 2)
    rope_cos = np.tile(np.repeat(np.cos(ang), 2, axis=-1), (1, N_KV_HEADS)).astype(f32)
    rope_sin = np.tile(np.repeat(np.sin(ang), 2, axis=-1) * sign, (1, N_KV_HEADS)).astype(f32)
    return (pmat, cc, cs, c_ctx, s_ctx, k1, k2), (tc, ts, rope_cos, rope_sin)


def kernel(x_prompt, x_sample, cache_k, cache_v, c, c_ctx, w_ada, b_ada, g_norm, w_in, g_q, g_k, conv_w, conv_b, conv_ln_g, conv_ln_b, sgu_g, w_s, b_s, w_out, w_up, ffn_conv_w, ffn_conv_b, w_down):
    bsz, seq, _ = x_prompt.shape
    dec_b, dec_s, _ = x_sample.shape
    past = cache_k.shape[2]
    assert dec_s == FFT_RADIX * FFT_RADIX and ROW_TILE % seq == 0 and dec_s % ROW_TILE == 0
    assert 1 + dec_b <= SUBLANES

    bf_consts, f32_consts = _np_consts(seq, dec_s)
    pmat, cc, cs, c_ctx_m, s_ctx_m, k1, k2 = (jnp.asarray(a).astype(BF16) for a in bf_consts)
    tc, ts, rope_cos, rope_sin = (jnp.asarray(a) for a in f32_consts)
    rope = (rope_cos, rope_sin)

    cond8 = jnp.zeros((SUBLANES, D_MODEL), F32).at[0].set(c_ctx).at[1:1 + dec_b].set(c)
    mod = _ada(cond8, w_ada, b_ada).reshape(DEPTH, SUBLANES, 6, D_MODEL)

    vec = lambda a: a.reshape(DEPTH, 1, -1)
    gq = vec(jnp.tile(g_q, (1, N_Q_HEADS)))
    gk = vec(jnp.tile(g_k, (1, N_KV_HEADS)))
    w_up_b = w_up.astype(BF16)
    w_dn_b = w_down.astype(BF16)
    ws_b = w_s.astype(BF16)
    cw_pad = jnp.pad(conv_w, ((0, 0), (0, CONV_TAP_ROWS - CONV_K), (0, 0)))
    fcw_pad = jnp.pad(ffn_conv_w, ((0, 0), (0, SUBLANES - FFN_CONV_K), (0, 0)))
    bs_full = jnp.repeat(jnp.swapaxes(b_s, 1, 2), SGU_HEAD_W, axis=2)
    ck = cache_k.reshape(dec_b, DEPTH, past, KV_W).astype(BF16)
    cvt = jnp.transpose(cache_v, (0, 1, 3, 4, 2))
    ones_row = jnp.zeros((dec_b, DEPTH, N_KV_HEADS, VT_ROWS - HEAD_DIM, past), F32).at[:, :, :, 0].set(1.0)
    cv = jnp.concatenate([cvt, ones_row], axis=3).astype(BF16)
    front_params = (mod, g_norm, w_in, gq, gk, pmat, vec(sgu_g), cc, cs)
    conv_params = (cw_pad, vec(conv_b), vec(conv_ln_g), vec(conv_ln_b))
    ffn_params = (mod, g_norm, w_up_b, fcw_pad, vec(ffn_conv_b), w_dn_b)

    xp = x_prompt.reshape(bsz * seq, D_MODEL)
    xs = x_sample.reshape(dec_b * dec_s, D_MODEL)
    ks, vs = [], []
    for l in range(DEPTH):
        q, k, v, ci, u, vn, gr, gi = _front(xp, *front_params, None, latent=False, seq=seq, layer=l)
        ks.append(k)
        vs.append(v)
        xp = _back_ctx(xp, mod, g_norm, q, k, v, ci, *conv_params, u, vn, ws_b, bs_full,
                       gr, gi, c_ctx_m, s_ctx_m, w_out, seq=seq, layer=l)
        xp = _ffn(xp, *ffn_params, seq=seq, layer=l, latent=False)

        q, k, v, ci, u, vn, gr, gi = _front(xs, *front_params, rope, latent=True, seq=dec_s, layer=l)
        grid4 = (dec_b, FFT_RADIX, FFT_RADIX, FNET_W)
        fn = _fft(gr.reshape(grid4), gi.reshape(grid4), k1, tc, ts, k2).reshape(-1, FNET_W)
        xs = _back_lat(xs, mod, g_norm, q, k, v, ck, cv, ci, *conv_params, u, vn, ws_b, bs_full,
                       fn, w_out, seq=dec_s, layer=l)
        xs = _ffn(xs, *ffn_params, seq=dec_s, layer=l, latent=True)

    kv_shape = (bsz, DEPTH, seq, N_KV_HEADS, HEAD_DIM)
    new_k = jnp.stack(ks, axis=1).reshape(kv_shape)
    new_v = jnp.stack(vs, axis=1).reshape(kv_shape)
    return (xp.reshape(x_prompt.shape), xs.reshape(x_sample.shape), new_k, new_v)
```
